```python
import math
import jax
import jax.numpy as jnp
from jax import lax

D_MODEL = 4096
BATCH = 8
SEQ = 2048
DEPTH = 2
DEC_BATCH = 32
DEC_SEQ = 32
PAST_LEN = 2048

CHUNK = 64
N_A_LAYERS = DEPTH // 2
N_B_LAYERS = DEPTH - N_A_LAYERS
HA_DK = 128
HA_HEADS = D_MODEL // HA_DK
HA_DV = D_MODEL // HA_HEADS
LA_BLOCK = 16
N_HEADS = 32
HEAD_DIM = D_MODEL // N_HEADS
BAND_CHUNKS = 8
BAND_ROWS = BAND_CHUNKS * CHUNK
REL_CLIP = 256
N_REL = REL_CLIP + CHUNK
N_EXPERTS = 32
TOP_K = 4
D_EXPERT = D_MODEL
SWIGLU_ALPHA = 1.702
SWIGLU_LIMIT = 7.0
MOE_BLOCK = 128
EPS = 1e-6
NEG_INF = -1e30

kernel_name = 'hybrid_stream_hgrn2_chunkband_moe_step'


def _rms_norm(x, g):
    xf = x.astype(jnp.float32)
    y = xf * lax.rsqrt(jnp.mean(xf * xf, axis=-1, keepdims=True) + EPS)
    return y.astype(x.dtype) * g


def _gla_chunked(q, k, v, log_f, S0):
    Bn, L, H, _ = q.shape
    V = v.shape[-1]
    n = -(-L // LA_BLOCK)
    pad = n * LA_BLOCK - L

    def blocks(t):
        t = jnp.pad(t, ((0, 0), (0, pad), (0, 0), (0, 0)))
        return t.reshape(Bn, n, LA_BLOCK, H, t.shape[-1]).transpose(1, 0, 3, 2, 4)

    qb, kb, vb, gb = blocks(q), blocks(k), blocks(v), blocks(log_f)
    G = jnp.cumsum(gb, axis=3)
    G_end = G[:, :, :, -1:, :]
    q_in = qb * jnp.exp(G)
    k_in = kb * jnp.exp(-G)
    k_end = kb * jnp.exp(G_end - G)
    decay = jnp.exp(G_end[:, :, :, 0, :])
    causal = jnp.tril(jnp.ones((LA_BLOCK, LA_BLOCK), jnp.float32))

    def step(S, blk):
        qi, ki, vi, ke, dk = blk
        a = jnp.einsum('bhtk,bhsk->bhts', qi, ki) * causal
        o = jnp.einsum('bhts,bhsv->bhtv', a, vi) + jnp.einsum('bhtk,bhkv->bhtv', qi, S)
        S = S * dk[..., None] + jnp.einsum('bhsk,bhsv->bhkv', ke, vi)
        return S, o

    S, o = lax.scan(step, S0, (q_in, k_in, vb, k_end, decay))
    o = o.transpose(1, 0, 3, 2, 4).reshape(Bn, n * LA_BLOCK, H, V)[:, :L]
    return o, S


def _hgrn2(h, w_in, w_out, g_norm, lb, S0):
    Bn, L, _ = h.shape
    q, f_pre, i, og = jnp.split(h @ w_in, 4, axis=-1)
    f = lb + (1.0 - lb) * jax.nn.sigmoid(f_pre.astype(jnp.float32))
    log_f = jnp.log(f)
    k = 1.0 - f
    q = q.astype(jnp.float32) * (HA_DK ** -0.5)
    o, S = _gla_chunked(q.reshape(Bn, L, HA_HEADS, HA_DK), k.reshape(Bn, L, HA_HEADS, HA_DK),
                        i.astype(jnp.float32).reshape(Bn, L, HA_HEADS, HA_DV),
                        log_f.reshape(Bn, L, HA_HEADS, HA_DK), S0.astype(jnp.float32))
    o = _rms_norm(o, g_norm) * jax.nn.silu(og.astype(jnp.float32).reshape(Bn, L, HA_HEADS, HA_DV))
    return o.reshape(Bn, L, D_MODEL).astype(h.dtype) @ w_out, S


def _rel_bias(table, T, J, offset):
    rel = jnp.arange(T)[:, None] + offset - jnp.arange(J)[None, :]
    idx = jnp.clip(rel, -(CHUNK - 1), REL_CLIP) + (CHUNK - 1)
    return table[:, idx].astype(jnp.float32)


def _band_attend(q, kb, vb, bias, valid):
    s = jnp.einsum('bthd,bjhd->bhtj', q, kb, preferred_element_type=jnp.float32) + bias
    if valid is not None:
        s = jnp.where(valid, s, NEG_INF)
    p = jax.nn.softmax(s, axis=-1).astype(vb.dtype)
    return jnp.einsum('bhtj,bjhd->bthd', p, vb)


def _band_attention_prompt(q, k, v, table):
    Bn, L, H, Dh = q.shape
    nc = L // CHUNK
    J = BAND_ROWS + CHUNK
    kp = jnp.pad(k, ((0, 0), (BAND_ROWS, 0), (0, 0), (0, 0)))
    vp = jnp.pad(v, ((0, 0), (BAND_ROWS, 0), (0, 0), (0, 0)))
    bias = _rel_bias(table, CHUNK, J, BAND_ROWS)
    qc = q.reshape(Bn, nc, CHUNK, H, Dh).transpose(1, 0, 2, 3, 4)
    band = jnp.arange(J)

    def one_chunk(args):
        ci, qn = args
        start = ci * CHUNK
        kb = lax.dynamic_slice_in_dim(kp, start, J, axis=1)
        vb = lax.dynamic_slice_in_dim(vp, start, J, axis=1)
        valid = start - BAND_ROWS + band >= 0
        return _band_attend(qn, kb, vb, bias, valid)

    o = lax.map(one_chunk, (jnp.arange(nc), qc))
    return o.transpose(1, 0, 2, 3, 4).reshape(Bn, L, H * Dh)


def _band_attention_sample(q, kb, vb, table):
    Bn, T, H, Dh = q.shape
    J = kb.shape[1]
    bias = _rel_bias(table, T, J, J - T)
    return _band_attend(q, kb, vb, bias, None).reshape(Bn, T, H * Dh)


def _moe(h, l, w_router, b_router, w_gate, b_gate, w_up, b_up, w_down, b_down):
    T, D = h.shape
    logits = (h @ w_router[l] + b_router[l]).astype(jnp.float32)
    top_v, top_e = lax.top_k(logits, TOP_K)
    gates = jax.nn.softmax(top_v, axis=-1).astype(h.dtype)
    n_assign = T * TOP_K
    e_flat = top_e.reshape(n_assign)
    tok_flat = jnp.arange(n_assign, dtype=jnp.int32) // TOP_K
    order = jnp.argsort(e_flat)
    e_sorted = e_flat[order]
    counts = jnp.bincount(e_flat, length=N_EXPERTS)
    padded = (counts + MOE_BLOCK - 1) // MOE_BLOCK * MOE_BLOCK
    start = jnp.cumsum(counts) - counts
    pend = jnp.cumsum(padded)
    pstart = pend - padded
    dest = pstart[e_sorted] + jnp.arange(n_assign, dtype=jnp.int32) - start[e_sorted]
    n_blocks = -(-n_assign // MOE_BLOCK) + N_EXPERTS
    n_rows = n_blocks * MOE_BLOCK
    row_tok = jnp.full((n_rows,), T, jnp.int32).at[dest].set(tok_flat[order])
    row_w = jnp.zeros((n_rows,), h.dtype).at[dest].set(gates.reshape(n_assign)[order])
    block_e = jnp.minimum(jnp.searchsorted(pend, jnp.arange(n_blocks) * MOE_BLOCK, side='right'),
                          N_EXPERTS - 1)
    h_pad = jnp.concatenate([h, jnp.zeros((1, D), h.dtype)], axis=0)

    def block(acc, blk):
        rows, wts, e = blk
        xb = h_pad[rows]
        g = jnp.minimum(xb @ w_gate[l, e] + b_gate[l, e], SWIGLU_LIMIT)
        u = jnp.clip(xb @ w_up[l, e] + b_up[l, e], -SWIGLU_LIMIT, SWIGLU_LIMIT)
        a = g * jax.nn.sigmoid(SWIGLU_ALPHA * g) * (u + 1.0)
        out = (a @ w_down[l, e] + b_down[l, e]) * wts[:, None]
        return acc.at[rows].add(out), None

    acc, _ = lax.scan(block, jnp.zeros((T + 1, D), h.dtype),
                      (row_tok.reshape(n_blocks, MOE_BLOCK), row_w.reshape(n_blocks, MOE_BLOCK), block_e))
    return acc[:T]


def _trunk(x, c, S0, cache_k, cache_v, p):
    Bn, L, _ = x.shape
    lb_all = jnp.cumsum(jax.nn.softmax(p['lb_param'].astype(jnp.float32), axis=0), axis=0)
    cond = jax.nn.silu(c)
    states = []
    k_sh = v_sh = k_band = v_band = None
    for l in range(DEPTH):
        mod = (cond @ p['w_ada'][l] + p['b_ada'][l])[:, None, :]
        sh1, sc1, gt1, sh2, sc2, gt2 = jnp.split(mod, 6, axis=-1)
        h = _rms_norm(x, p['norm_mix'][l]) * (1.0 + sc1) + sh1
        if l < N_A_LAYERS:
            o, S = _hgrn2(h, p['w_in_a'][l], p['w_out_a'][l], p['gnorm_a'][l], lb_all[l], S0[l])
            states.append(S.astype(S0.dtype))
        else:
            j = l - N_A_LAYERS
            q = (h @ p['w_q_b'][j]).reshape(Bn, L, N_HEADS, HEAD_DIM)
            q = _rms_norm(q, p['q_norm'][j]) * (HEAD_DIM ** -0.5)
            if cache_k is None:
                a = _band_attention_prompt(q, k_sh, v_sh, p['rel_bias'][j])
            else:
                a = _band_attention_sample(q, k_band, v_band, p['rel_bias'][j])
            o = a @ p['w_out_b'][j]
        x = x + gt1 * o
        h = _rms_norm(x, p['norm_ffn'][l]) * (1.0 + sc2) + sh2
        y = _moe(h.reshape(Bn * L, D_MODEL), l, p['w_router'], p['b_router'], p['w_gate'], p['b_gate'],
                 p['w_up'], p['b_up'], p['w_down'], p['b_down'])
        x = x + gt2 * y.reshape(Bn, L, D_MODEL)
        if l == N_A_LAYERS - 1:
            k_sh, v_sh = jnp.split(_rms_norm(x, p['kv_norm']) @ p['w_kv'], 2, axis=-1)
            k_sh = _rms_norm(k_sh.reshape(Bn, L, N_HEADS, HEAD_DIM), p['k_norm'])
            v_sh = v_sh.reshape(Bn, L, N_HEADS, HEAD_DIM)
            if cache_k is not None:
                k_band = jnp.concatenate([cache_k.astype(k_sh.dtype), k_sh], axis=1)
                v_band = jnp.concatenate([cache_v.astype(v_sh.dtype), v_sh], axis=1)
    return x, jnp.stack(states), k_sh, v_sh


def _w(key, shape, std):
    a = std * math.sqrt(3.0)
    return jax.random.uniform(key, shape, jnp.float32, -a, a)


def setup_inputs(seed: int = 0) -> dict:
    key = jax.random.key(seed)
    ks = jax.random.split(key, 32)
    D, E, F = D_MODEL, N_EXPERTS, D_EXPERT
    rows = min(BAND_ROWS, PAST_LEN)
    return {
        'x_prompt': jax.random.normal(ks[0], (BATCH, SEQ, D), jnp.float32),
        'x_sample': jax.random.normal(ks[1], (DEC_BATCH, DEC_SEQ, D), jnp.float32),
        'c_prompt': jax.random.normal(ks[2], (BATCH, D), jnp.float32),
        'c_sample': jax.random.normal(ks[3], (DEC_BATCH, D), jnp.float32),
        'state_hgrn': 0.5 * jax.random.normal(ks[4], (N_A_LAYERS, DEC_BATCH, HA_HEADS, HA_DK, HA_DV), jnp.float32),
        'cache_k': jax.random.normal(ks[5], (DEC_BATCH, rows, N_HEADS, HEAD_DIM), jnp.float32),
        'cache_v': jax.random.normal(ks[6], (DEC_BATCH, rows, N_HEADS, HEAD_DIM), jnp.float32),
        'norm_mix': 1.0 + _w(ks[7], (DEPTH, D), 0.02),
        'norm_ffn': 1.0 + _w(ks[8], (DEPTH, D), 0.02),
        'w_ada': _w(ks[9], (DEPTH, D, 6 * D), 0.5 * D ** -0.5),
        'b_ada': _w(ks[10], (DEPTH, 6 * D), 0.01),
        'w_in_a': _w(ks[11], (N_A_LAYERS, D, 4 * D), D ** -0.5),
        'lb_param': 1.0 + _w(ks[12], (N_A_LAYERS + 1, D), 0.1),
        'gnorm_a': 1.0 + _w(ks[13], (N_A_LAYERS, HA_DV), 0.02),
        'w_out_a': _w(ks[14], (N_A_LAYERS, D, D), D ** -0.5),
        'kv_norm': 1.0 + _w(ks[15], (D,), 0.02),
        'w_kv': _w(ks[16], (D, 2 * D), D ** -0.5),
        'k_norm': 1.0 + _w(ks[17], (HEAD_DIM,), 0.02),
        'w_q_b': _w(ks[18], (N_B_LAYERS, D, D), D ** -0.5),
        'q_norm': 1.0 + _w(ks[19], (N_B_LAYERS, HEAD_DIM), 0.02),
        'rel_bias': _w(ks[20], (N_B_LAYERS, N_HEADS, N_REL), 0.2),
        'w_out_b': _w(ks[21], (N_B_LAYERS, D, D), D ** -0.5),
        'w_router': _w(ks[22], (DEPTH, D, E), D ** -0.5),
        'b_router': _w(ks[23], (DEPTH, E), 0.01),
        'w_gate': _w(ks[24], (DEPTH, E, D, F), D ** -0.5),
        'b_gate': _w(ks[25], (DEPTH, E, F), 0.01),
        'w_up': _w(ks[26], (DEPTH, E, D, F), D ** -0.5),
        'b_up': _w(ks[27], (DEPTH, E, F), 0.01),
        'w_down': _w(ks[28], (DEPTH, E, F, D), F ** -0.5),
        'b_down': _w(ks[29], (DEPTH, E, D), 0.01),
    }


def reference(x_prompt, x_sample, c_prompt, c_sample, state_hgrn, cache_k, cache_v,
              norm_mix, norm_ffn, w_ada, b_ada, w_in_a, lb_param, gnorm_a, w_out_a,
              kv_norm, w_kv, k_norm, w_q_b, q_norm, rel_bias, w_out_b,
              w_router, b_router, w_gate, b_gate, w_up, b_up, w_down, b_down):
    p = {'norm_mix': norm_mix, 'norm_ffn': norm_ffn, 'w_ada': w_ada, 'b_ada': b_ada,
         'w_in_a': w_in_a, 'lb_param': lb_param, 'gnorm_a': gnorm_a, 'w_out_a': w_out_a,
         'kv_norm': kv_norm, 'w_kv': w_kv, 'k_norm': k_norm, 'w_q_b': w_q_b, 'q_norm': q_norm,
         'rel_bias': rel_bias, 'w_out_b': w_out_b, 'w_router': w_router, 'b_router': b_router,
         'w_gate': w_gate, 'b_gate': b_gate, 'w_up': w_up, 'b_up': b_up,
         'w_down': w_down, 'b_down': b_down}
    s0_prompt = jnp.zeros((N_A_LAYERS, x_prompt.shape[0], HA_HEADS, HA_DK, HA_DV), state_hgrn.dtype)
    y_prompt, state_hgrn_prompt, k_p, v_p = _trunk(x_prompt, c_prompt, s0_prompt, None, None, p)
    rows = min(BAND_ROWS, x_prompt.shape[1])
    k_prompt = k_p[:, k_p.shape[1] - rows:]
    v_prompt = v_p[:, v_p.shape[1] - rows:]
    y_sample, state_hgrn_sample, k_sample, v_sample = _trunk(x_sample, c_sample, state_hgrn, cache_k, cache_v, p)
    return (y_prompt, y_sample, state_hgrn_prompt, state_hgrn_sample, k_prompt, v_prompt, k_sample, v_sample)
```

```python
import functools

import jax
import jax.numpy as jnp
from jax import lax
from jax.experimental import pallas as pl
from jax.experimental.pallas import tpu as pltpu

CHUNK = 64
BAND_ROWS = 8 * CHUNK
REL_CLIP = 256
HEAD_DIM = 128
SUB = 16
TOP_K = 4
SWIGLU_ALPHA = 1.702
SWIGLU_LIMIT = 7.0
EPS = 1e-6
NEG_INF = -1e30

V7X_LANES = 128
V7X_VMEM_BYTES = 64 * 1024 * 1024
V7X_VMEM_CAP = V7X_VMEM_BYTES - 8 * 1024 * 1024
VMEM_SLACK = 6 * 1024 * 1024

BF16 = jnp.bfloat16
F32 = jnp.float32


def _nbytes(shape, dtype):
    n = 1
    for s in shape:
        n *= s
    return n * jnp.dtype(dtype).itemsize


def _params(semantics, block_bytes, scratch_bytes=0):
    limit = min(2 * block_bytes + scratch_bytes + VMEM_SLACK, V7X_VMEM_CAP)
    return pltpu.CompilerParams(dimension_semantics=semantics, vmem_limit_bytes=int(limit))


def _sigmoid(x):
    return 1.0 / (1.0 + jnp.exp(-x))


def _dot(a, b):
    return jnp.dot(a, b, preferred_element_type=F32)


def _dot_nt(a, b):
    return lax.dot_general(a, b, (((1,), (1,)), ((), ())), preferred_element_type=F32)


def _dot_tn(a, b):
    return lax.dot_general(a, b, (((0,), (0,)), ((), ())), preferred_element_type=F32)


def _mm_bias_kernel(x_ref, w_ref, b_ref, o_ref):
    o_ref[...] = _dot(x_ref[...], w_ref[...]) + b_ref[...]


def _mm_plain_kernel(x_ref, w_ref, *o_refs):
    y = _dot(x_ref[...], w_ref[...])
    for o_ref in o_refs:
        o_ref[...] = y.astype(o_ref.dtype)


def _mm_resgate_kernel(x_ref, w_ref, res_ref, g_ref, o_ref):
    o_ref[...] = res_ref[...] + g_ref[...] * _dot(x_ref[...], w_ref[...])


def _mm_headnorm_kernel(x_ref, w_ref, gn_ref, *o_refs, scale):
    y = _dot(x_ref[...], w_ref[...])
    gn = gn_ref[...]
    for hh in range(y.shape[1] // HEAD_DIM):
        sl = slice(hh * HEAD_DIM, (hh + 1) * HEAD_DIM)
        seg = y[:, sl]
        ms = jnp.mean(seg * seg, axis=-1, keepdims=True)
        z = seg * lax.rsqrt(ms + EPS) * gn
        if scale != 1.0:
            z = z * scale
        for o_ref in o_refs:
            o_ref[:, sl] = z.astype(o_ref.dtype)


def _matmul(body, x, w, extras, extra_specs, out_dtypes, *, tm, tn, name):
    M, K = x.shape
    N = w.shape[1]
    assert M % tm == 0 and N % tn == 0, (M, N, tm, tn)
    in_specs = [pl.BlockSpec((tm, K), lambda i, j: (i, 0)),
                pl.BlockSpec((K, tn), lambda i, j: (0, j))] + list(extra_specs)
    out_specs = [pl.BlockSpec((tm, tn), lambda i, j: (i, j)) for _ in out_dtypes]
    out_shape = [jax.ShapeDtypeStruct((M, N), dt) for dt in out_dtypes]
    block_bytes = (_nbytes((tm, K), x.dtype) + _nbytes((K, tn), w.dtype)
                   + sum(_nbytes((tm, tn), dt) for dt in out_dtypes)
                   + len(extras) * _nbytes((tm, tn), F32))
    outs = pl.pallas_call(
        body, grid=(M // tm, N // tn), in_specs=in_specs, out_specs=out_specs, out_shape=out_shape,
        compiler_params=_params(("parallel", "arbitrary"), block_bytes), name=name,
    )(x, w, *extras)
    return outs


MM_TM = 1024
MM_TN = 512


def _gate_spec(gate, tm, tn, rows_per_seq):
    if gate.ndim == 3:
        assert rows_per_seq % tm == 0
        per = rows_per_seq // tm
        return pl.BlockSpec((None, 1, tn), lambda i, j: (i // per, 0, j))
    return pl.BlockSpec((tm, tn), lambda i, j: (i, j))


def _mm_tiles(B, L, per_sequence_gate):
    rows = L if per_sequence_gate else B * L
    tm = MM_TM if rows % MM_TM == 0 else rows
    return tm, MM_TN


def _norm_mod(x_ref, g_ref, sc_ref, sh_ref):
    x = x_ref[...]
    ms = jnp.mean(x * x, axis=-1, keepdims=True)
    y = x * lax.rsqrt(ms + EPS) * g_ref[...]
    return y * (1.0 + sc_ref[...]) + sh_ref[...]


def _norm_kernel(x_ref, g_ref, sc_ref, sh_ref, h_ref):
    h_ref[...] = _norm_mod(x_ref, g_ref, sc_ref, sh_ref).astype(h_ref.dtype)


def _norm_router_kernel(x_ref, g_ref, sc_ref, sh_ref, wr_ref, br_ref, h_ref, gate_ref, idx_ref, *, n_experts):
    h = _norm_mod(x_ref, g_ref, sc_ref, sh_ref)
    h_ref[...] = h
    hi = h.astype(BF16)
    lo = (h - hi.astype(F32)).astype(BF16)
    w = wr_ref[...]
    whi = w.astype(BF16)
    wlo = (w - whi.astype(F32)).astype(BF16)
    logits = _dot(hi, whi) + _dot(hi, wlo) + _dot(lo, whi) + br_ref[...]
    lane = lax.broadcasted_iota(jnp.int32, logits.shape, 1)
    neg = jnp.float32(-jnp.inf)
    l = jnp.where(lane < n_experts, logits, neg)
    vals, idxs = [], []
    for _ in range(TOP_K):
        m = jnp.max(l, axis=-1, keepdims=True)
        idx = jnp.min(jnp.where(l == m, lane, V7X_LANES), axis=-1, keepdims=True)
        vals.append(m)
        idxs.append(idx)
        l = jnp.where(lane == idx, neg, l)
    exps = [jnp.exp(v - vals[0]) for v in vals]
    denom = exps[0]
    for e in exps[1:]:
        denom = denom + e
    gates = jnp.zeros(logits.shape, F32)
    sel = jnp.zeros(logits.shape, jnp.int32)
    for k in range(TOP_K):
        gates = jnp.where(lane == k, exps[k] / denom, gates)
        sel = jnp.where(lane == k, idxs[k], sel)
    gate_ref[...] = gates
    idx_ref[...] = sel


def _norm_rows(L):
    return 256 if L % 256 == 0 else L


def _norm_specs(B, L, D, tl):
    nl = L // tl
    x_spec = pl.BlockSpec((tl, D), lambda b, l: (b * nl + l, 0))
    g_spec = pl.BlockSpec((1, D), lambda b, l: (0, 0))
    m_spec = pl.BlockSpec((None, 1, D), lambda b, l: (b, 0, 0))
    return x_spec, g_spec, m_spec


def _norm(x, g, sc, sh, B, L, out_dtype):
    D = x.shape[1]
    tl = _norm_rows(L)
    x_spec, g_spec, m_spec = _norm_specs(B, L, D, tl)
    block_bytes = _nbytes((tl, D), F32) + _nbytes((tl, D), out_dtype) + 3 * _nbytes((1, D), F32)
    return pl.pallas_call(
        _norm_kernel, grid=(B, L // tl),
        in_specs=[x_spec, g_spec, m_spec, m_spec], out_specs=x_spec,
        out_shape=jax.ShapeDtypeStruct(x.shape, out_dtype),
        compiler_params=_params(("parallel", "parallel"), block_bytes), name="norm_mod",
    )(x, g.reshape(1, D), sc.reshape(B, 1, D), sh.reshape(B, 1, D))


def _norm_router(x, g, sc, sh, w_router, b_router, B, L):
    M, D = x.shape
    E = w_router.shape[1]
    tl = _norm_rows(L)
    x_spec, g_spec, m_spec = _norm_specs(B, L, D, tl)
    wr = jnp.pad(w_router, ((0, 0), (0, V7X_LANES - E)))
    br = jnp.pad(b_router, (0, V7X_LANES - E)).reshape(1, V7X_LANES)
    r_spec = pl.BlockSpec((tl, V7X_LANES), lambda b, l: (b * (L // tl) + l, 0))
    block_bytes = (2 * _nbytes((tl, D), F32) + 3 * _nbytes((1, D), F32) + _nbytes((D, V7X_LANES), F32)
                   + 2 * _nbytes((tl, V7X_LANES), F32))
    return pl.pallas_call(
        functools.partial(_norm_router_kernel, n_experts=E), grid=(B, L // tl),
        in_specs=[x_spec, g_spec, m_spec, m_spec,
                  pl.BlockSpec((D, V7X_LANES), lambda b, l: (0, 0)),
                  pl.BlockSpec((1, V7X_LANES), lambda b, l: (0, 0))],
        out_specs=[x_spec, r_spec, r_spec],
        out_shape=[jax.ShapeDtypeStruct((M, D), F32),
                   jax.ShapeDtypeStruct((M, V7X_LANES), F32),
                   jax.ShapeDtypeStruct((M, V7X_LANES), jnp.int32)],
        compiler_params=_params(("parallel", "parallel"), block_bytes), name="norm_router",
    )(x, g.reshape(1, D), sc.reshape(B, 1, D), sh.reshape(B, 1, D), wr, br)


def _hgrn_kernel(q_ref, f_ref, i_ref, og_ref, lb_ref, gn_ref, s0_ref, o_ref, sout_ref, s_scr, *, C, Hg):
    c = pl.program_id(2)
    nsub = C // SUB
    W = Hg * HEAD_DIM

    @pl.when(c == 0)
    def _():
        s_scr[...] = s0_ref[...]

    lb = lb_ref[...]
    f = lb + (1.0 - lb) * _sigmoid(f_ref[...])
    logf = jnp.log(f)
    kk = 1.0 - f
    row = lax.broadcasted_iota(jnp.int32, (C, C), 0)
    col = lax.broadcasted_iota(jnp.int32, (C, C), 1)
    tri = jnp.where(col <= row, 1.0, 0.0).astype(BF16)
    a1 = logf.astype(BF16)
    r1 = logf - a1.astype(F32)
    a2 = r1.astype(BF16)
    a3 = (r1 - a2.astype(F32)).astype(BF16)
    G = _dot(tri, a1) + _dot(tri, a2) + _dot(tri, a3)

    q = q_ref[...] * (HEAD_DIM ** -0.5)
    v = i_ref[...].astype(BF16)
    og = og_ref[...]
    gn = gn_ref[...]
    ends = [G[SUB * j + SUB - 1:SUB * j + SUB, :] for j in range(nsub)]
    bblk = jnp.concatenate([jnp.broadcast_to(e, (SUB, W)) for e in ends], axis=0)
    kt = kk * jnp.exp(bblk - G)
    k_end = (kt * jnp.exp(ends[-1] - bblk)).astype(BF16)
    kt = kt.astype(BF16)
    q_in = (q * jnp.exp(G)).astype(BF16)
    decay = jnp.exp(ends[-1])

    for h in range(Hg):
        sl = slice(h * HEAD_DIM, (h + 1) * HEAD_DIM)
        S = s_scr[h]
        o = _dot(q_in[:, sl], S.astype(BF16))
        for j in range(nsub):
            r0 = SUB * j
            qj = (q[r0:, sl] * jnp.exp(G[r0:, sl] - ends[j][:, sl])).astype(BF16)
            a = _dot_nt(qj, kt[r0:r0 + SUB, sl])
            rr = lax.broadcasted_iota(jnp.int32, a.shape, 0)
            ss = lax.broadcasted_iota(jnp.int32, a.shape, 1)
            a = jnp.where(rr >= ss, a, 0.0)
            part = _dot(a.astype(BF16), v[r0:r0 + SUB, sl])
            if r0:
                part = jnp.concatenate([jnp.zeros((r0, HEAD_DIM), F32), part], axis=0)
            o = o + part
        dcol = jnp.broadcast_to(decay[:, sl], (HEAD_DIM, HEAD_DIM)).T
        S_new = S * dcol + _dot_tn(k_end[:, sl], v[:, sl])
        s_scr[h] = S_new
        sout_ref[h] = S_new
        ms = jnp.mean(o * o, axis=-1, keepdims=True)
        ogh = og[:, sl]
        y = o * lax.rsqrt(ms + EPS) * gn * (ogh * _sigmoid(ogh))
        o_ref[:, sl] = y.astype(o_ref.dtype)


def _hgrn(proj, lb, gnorm, s0, B, L):
    M, D4 = proj.shape
    D = D4 // 4
    H = D // HEAD_DIM
    C = CHUNK if L % CHUNK == 0 else L
    assert L % C == 0 and C % SUB == 0
    Hg = 4
    W = Hg * HEAD_DIM
    nc, ng = L // C, H // Hg

    def sec_spec(sec):
        return pl.BlockSpec((C, W), lambda b, g, c: (b * nc + c, sec * ng + g))

    s_spec = pl.BlockSpec((None, Hg, HEAD_DIM, HEAD_DIM), lambda b, g, c: (b, g, 0, 0))
    block_bytes = 4 * _nbytes((C, W), F32) + _nbytes((C, W), BF16) + 2 * _nbytes((Hg, HEAD_DIM, HEAD_DIM), F32)
    o, s_out = pl.pallas_call(
        functools.partial(_hgrn_kernel, C=C, Hg=Hg), grid=(B, ng, nc),
        in_specs=[sec_spec(0), sec_spec(1), sec_spec(2), sec_spec(3),
                  pl.BlockSpec((1, W), lambda b, g, c: (0, g)),
                  pl.BlockSpec((1, HEAD_DIM), lambda b, g, c: (0, 0)),
                  s_spec],
        out_specs=[pl.BlockSpec((C, W), lambda b, g, c: (b * nc + c, g)), s_spec],
        out_shape=[jax.ShapeDtypeStruct((M, D), BF16),
                   jax.ShapeDtypeStruct((B, H, HEAD_DIM, HEAD_DIM), F32)],
        scratch_shapes=[pltpu.VMEM((Hg, HEAD_DIM, HEAD_DIM), F32)],
        compiler_params=_params(("parallel", "parallel", "arbitrary"), block_bytes,
                                _nbytes((Hg, HEAD_DIM, HEAD_DIM), F32)),
        name="hgrn2",
    )(proj, proj, proj, proj, lb.reshape(1, D), gnorm.reshape(1, HEAD_DIM), s0)
    return o, s_out


QT = 2 * CHUNK
BAND_T = BAND_ROWS + QT


def _attn_prompt_kernel(q_ref, k_ref, v_ref, bias_ref, o_ref, kpad, vpad):
    L = q_ref.shape[0]
    kpad[0:BAND_ROWS, :] = jnp.zeros((BAND_ROWS, HEAD_DIM), BF16)
    vpad[0:BAND_ROWS, :] = jnp.zeros((BAND_ROWS, HEAD_DIM), BF16)
    kpad[BAND_ROWS:, :] = k_ref[...]
    vpad[BAND_ROWS:, :] = v_ref[...]
    t = lax.broadcasted_iota(jnp.int32, (QT, BAND_T), 0)
    j = lax.broadcasted_iota(jnp.int32, (QT, BAND_T), 1)
    first = jnp.where(t >= CHUNK, CHUNK, 0)
    in_band = (j >= first) & (j < first + BAND_ROWS + CHUNK)
    bias = jnp.where(in_band, bias_ref[...], NEG_INF)

    def tile(i, carry):
        start = pl.multiple_of(i * QT, QT)
        q = q_ref[pl.ds(start, QT), :]
        kb = kpad[pl.ds(start, BAND_T), :]
        vb = vpad[pl.ds(start, BAND_T), :]
        s = _dot_nt(q, kb) + bias
        s = jnp.where(j >= BAND_ROWS - i * QT, s, NEG_INF)
        m = jnp.max(s, axis=-1, keepdims=True)
        p = jnp.exp(s - m)
        l = jnp.sum(p, axis=-1, keepdims=True)
        o = _dot(p.astype(BF16), vb) / l
        o_ref[pl.ds(start, QT), :] = o.astype(o_ref.dtype)
        return carry

    lax.fori_loop(0, L // QT, tile, 0)


def _attn_prompt(q, k, v, bias, B, L):
    M, D = q.shape
    H = D // HEAD_DIM
    assert L % QT == 0
    spec = pl.BlockSpec((L, HEAD_DIM), lambda b, h: (b, h))
    block_bytes = 4 * _nbytes((L, HEAD_DIM), BF16) + _nbytes((QT, BAND_T), F32)
    scratch = 2 * _nbytes((L + BAND_ROWS, HEAD_DIM), BF16)
    return pl.pallas_call(
        _attn_prompt_kernel, grid=(B, H),
        in_specs=[spec, spec, spec, pl.BlockSpec((None, QT, BAND_T), lambda b, h: (h, 0, 0))],
        out_specs=spec, out_shape=jax.ShapeDtypeStruct((M, D), BF16),
        scratch_shapes=[pltpu.VMEM((L + BAND_ROWS, HEAD_DIM), BF16)] * 2,
        compiler_params=_params(("parallel", "parallel"), block_bytes, scratch), name="attn_prompt",
    )(q, k, v, bias)


def _attn_sample_kernel(q_ref, ck_ref, cv_ref, kn_ref, vn_ref, bias_ref, o_ref, *, Hg):
    T = q_ref.shape[0]
    J0 = ck_ref.shape[0]
    for h in range(Hg):
        sl = slice(h * HEAD_DIM, (h + 1) * HEAD_DIM)
        q = q_ref[:, sl]
        s1 = _dot_nt(q, ck_ref[:, sl].astype(BF16)) + bias_ref[h, 0:T, 0:J0]
        s2 = _dot_nt(q, kn_ref[:, sl]) + bias_ref[h, 0:T, J0:J0 + T]
        m = jnp.maximum(jnp.max(s1, axis=-1, keepdims=True), jnp.max(s2, axis=-1, keepdims=True))
        p1 = jnp.exp(s1 - m)
        p2 = jnp.exp(s2 - m)
        l = jnp.sum(p1, axis=-1, keepdims=True) + jnp.sum(p2, axis=-1, keepdims=True)
        o = _dot(p1.astype(BF16), cv_ref[:, sl].astype(BF16)) + _dot(p2.astype(BF16), vn_ref[:, sl])
        o_ref[:, sl] = (o / l).astype(o_ref.dtype)


def _attn_sample(q, cache_k, cache_v, k_new, v_new, bias, B, T):
    M, D = q.shape
    H = D // HEAD_DIM
    J0 = cache_k.shape[1]
    assert J0 == BAND_ROWS and T <= QT and J0 + T <= BAND_T
    Hg = 8
    W = Hg * HEAD_DIM
    n_spec = pl.BlockSpec((T, W), lambda b, g: (b, g))
    c_spec = pl.BlockSpec((None, J0, W), lambda b, g: (b, 0, g))
    block_bytes = (3 * _nbytes((T, W), BF16) + 2 * _nbytes((J0, W), F32) + _nbytes((Hg, QT, BAND_T), F32)
                   + _nbytes((T, W), BF16))
    return pl.pallas_call(
        functools.partial(_attn_sample_kernel, Hg=Hg), grid=(B, H // Hg),
        in_specs=[n_spec, c_spec, c_spec, n_spec, n_spec,
                  pl.BlockSpec((Hg, QT, BAND_T), lambda b, g: (g, 0, 0))],
        out_specs=n_spec, out_shape=jax.ShapeDtypeStruct((M, D), BF16),
        compiler_params=_params(("parallel", "parallel"), block_bytes), name="attn_sample",
    )(q, cache_k, cache_v, k_new, v_new, bias)


def _rel_bias_tile(table):
    rel = jnp.arange(QT)[:, None] + BAND_ROWS - jnp.arange(BAND_T)[None, :]
    idx = jnp.clip(rel, -(CHUNK - 1), REL_CLIP) + (CHUNK - 1)
    return table[:, idx].astype(F32)


MOE_TM = 512
GATHER_ROWS = 256
COMBINE_ROWS = 128


def _gather_kernel(idx_ref, nrows_ref, h_hbm, o_ref, buf, sem):
    R = o_ref.shape[0]
    base = pl.program_id(0) * R

    def row_copy(r, src_row):
        return pltpu.make_async_copy(h_hbm.at[pl.ds(src_row, 1)], buf.at[pl.ds(r, 1)], sem)

    @pl.when(base < nrows_ref[0])
    def _():
        def start(r, c):
            row_copy(r, idx_ref[base + r]).start()
            return c

        def wait(r, c):
            row_copy(r, 0).wait()
            return c

        lax.fori_loop(0, R, start, 0)
        lax.fori_loop(0, R, wait, 0)
        o_ref[...] = buf[...].astype(o_ref.dtype)

    @pl.when(base >= nrows_ref[0])
    def _():
        o_ref[...] = jnp.zeros(o_ref.shape, o_ref.dtype)


def _gather_rows(h, row_tok, n_used_rows):
    n_rows = row_tok.shape[0]
    D = h.shape[1]
    R = GATHER_ROWS
    assert n_rows % R == 0
    grid_spec = pltpu.PrefetchScalarGridSpec(
        num_scalar_prefetch=2, grid=(n_rows // R,),
        in_specs=[pl.BlockSpec(memory_space=pl.ANY)],
        out_specs=pl.BlockSpec((R, D), lambda i, idx, n: (i, 0)),
        scratch_shapes=[pltpu.VMEM((R, D), F32), pltpu.SemaphoreType.DMA(())])
    return pl.pallas_call(
        _gather_kernel, grid_spec=grid_spec, out_shape=jax.ShapeDtypeStruct((n_rows, D), BF16),
        compiler_params=_params(("arbitrary",), _nbytes((R, D), BF16), _nbytes((R, D), F32)),
        name="moe_gather",
    )(row_tok, n_used_rows, h)


def _moe_up_kernel(be_ref, first_ref, nblk_ref, x_ref, wg_ref, wu_ref, bg_ref, bu_ref, a_ref, wg_s, wu_s):
    i = pl.program_id(1)

    @pl.when(i < nblk_ref[0])
    def _():
        @pl.when(first_ref[i] == 1)
        def _():
            wg_s[...] = wg_ref[...].astype(BF16)
            wu_s[...] = wu_ref[...].astype(BF16)

        x = x_ref[...]
        g = jnp.minimum(_dot(x, wg_s[...]) + bg_ref[...], SWIGLU_LIMIT)
        u = jnp.clip(_dot(x, wu_s[...]) + bu_ref[...], -SWIGLU_LIMIT, SWIGLU_LIMIT)
        a_ref[...] = (g * _sigmoid(SWIGLU_ALPHA * g) * (u + 1.0)).astype(a_ref.dtype)

    @pl.when(i >= nblk_ref[0])
    def _():
        a_ref[...] = jnp.zeros(a_ref.shape, a_ref.dtype)


def _moe_down_kernel(be_ref, first_ref, nblk_ref, a_ref, wd_ref, bd_ref, y_ref, wd_s):
    i = pl.program_id(1)

    @pl.when(i < nblk_ref[0])
    def _():
        @pl.when(first_ref[i] == 1)
        def _():
            wd_s[...] = wd_ref[...].astype(BF16)

        y_ref[...] = _dot(a_ref[...], wd_s[...]) + bd_ref[...]

    @pl.when(i >= nblk_ref[0])
    def _():
        y_ref[...] = jnp.zeros(y_ref.shape, y_ref.dtype)


def _moe_grouped(body, x, weights, biases, layer, blk_e, blk_first, n_blk, out_dtype, tn, name):
    n_rows, K = x.shape
    N = weights[0].shape[-1]
    tm = MOE_TM
    n_blocks = n_rows // tm

    def x_map(j, i, be, fi, nb):
        return (jnp.minimum(i, nb[0] - 1), 0)

    def w_map(j, i, be, fi, nb):
        return (layer, be[i], 0, j)

    in_specs = [pl.BlockSpec((tm, K), x_map)]
    in_specs += [pl.BlockSpec((None, None, K, tn), w_map) for _ in weights]
    in_specs += [pl.BlockSpec((None, None, 1, tn), w_map) for _ in biases]
    grid_spec = pltpu.PrefetchScalarGridSpec(
        num_scalar_prefetch=3, grid=(N // tn, n_blocks), in_specs=in_specs,
        out_specs=pl.BlockSpec((tm, tn), lambda j, i, be, fi, nb: (i, j)),
        scratch_shapes=[pltpu.VMEM((K, tn), BF16) for _ in weights])
    block_bytes = (_nbytes((tm, K), x.dtype) + len(weights) * _nbytes((K, tn), F32)
                   + _nbytes((tm, tn), out_dtype) + len(biases) * _nbytes((8, tn), F32))
    scratch = len(weights) * _nbytes((K, tn), BF16)
    b4 = [b.reshape(b.shape[0], b.shape[1], 1, b.shape[2]) for b in biases]
    return pl.pallas_call(
        body, grid_spec=grid_spec, out_shape=jax.ShapeDtypeStruct((n_rows, N), out_dtype),
        compiler_params=_params(("arbitrary", "arbitrary"), block_bytes, scratch), name=name,
    )(blk_e, blk_first, n_blk, x, *weights, *b4)


def _combine_kernel(pos_ref, y_hbm, gates_ref, x_ref, gt_ref, o_ref, buf, sem):
    R = x_ref.shape[0]
    base = pl.program_id(0) * R

    def row_copy(k, r, src_row):
        return pltpu.make_async_copy(y_hbm.at[pl.ds(src_row, 1)], buf.at[k, pl.ds(r, 1)], sem)

    def start(r, c):
        for k in range(TOP_K):
            row_copy(k, r, pos_ref[(base + r) * TOP_K + k]).start()
        return c

    def wait(r, c):
        for k in range(TOP_K):
            row_copy(k, r, 0).wait()
        return c

    lax.fori_loop(0, R, start, 0)
    lax.fori_loop(0, R, wait, 0)
    gates = gates_ref[...]
    moe = gates[:, 0:1] * buf[0]
    for k in range(1, TOP_K):
        moe = moe + gates[:, k:k + 1] * buf[k]
    o_ref[...] = x_ref[...] + gt_ref[...] * moe


def _combine(y_sorted, pos, gates, x, gt, rows_per_seq):
    T, D = x.shape
    R = COMBINE_ROWS if T % COMBINE_ROWS == 0 else T
    if gt.ndim == 3:
        assert rows_per_seq % R == 0
        per = rows_per_seq // R
        gt_spec = pl.BlockSpec((None, 1, D), lambda i, p: (i // per, 0, 0))
    else:
        gt_spec = pl.BlockSpec((R, D), lambda i, p: (i, 0))
    row_spec = pl.BlockSpec((R, D), lambda i, p: (i, 0))
    grid_spec = pltpu.PrefetchScalarGridSpec(
        num_scalar_prefetch=1, grid=(T // R,),
        in_specs=[pl.BlockSpec(memory_space=pl.ANY),
                  pl.BlockSpec((R, V7X_LANES), lambda i, p: (i, 0)), row_spec, gt_spec],
        out_specs=row_spec,
        scratch_shapes=[pltpu.VMEM((TOP_K, R, D), F32), pltpu.SemaphoreType.DMA(())])
    block_bytes = 3 * _nbytes((R, D), F32) + _nbytes((R, V7X_LANES), F32)
    return pl.pallas_call(
        _combine_kernel, grid_spec=grid_spec, out_shape=jax.ShapeDtypeStruct((T, D), F32),
        compiler_params=_params(("arbitrary",), block_bytes, _nbytes((TOP_K, R, D), F32)),
        name="moe_combine",
    )(pos, y_sorted, gates, x, gt)


def _route(top_e, n_experts):
    A = top_e.shape[0] * TOP_K
    tm = MOE_TM
    e_flat = top_e.reshape(A)
    onehot = (e_flat[:, None] == jnp.arange(n_experts, dtype=jnp.int32)[None, :]).astype(jnp.int32)
    csum = jnp.cumsum(onehot, axis=0)
    rank = jnp.take_along_axis(csum, e_flat[:, None], axis=1)[:, 0] - 1
    counts = csum[-1]
    padded = (counts + tm - 1) // tm * tm
    pend = jnp.cumsum(padded)
    pstart = pend - padded
    pos = (pstart[e_flat] + rank).astype(jnp.int32)
    n_blocks = -(-A // tm) + n_experts
    n_blocks = -(-n_blocks * tm // GATHER_ROWS) * GATHER_ROWS // tm
    n_rows = n_blocks * tm
    row_tok = jnp.zeros((n_rows,), jnp.int32).at[pos].set(jnp.arange(A, dtype=jnp.int32) // TOP_K)
    blk_start = jnp.arange(n_blocks, dtype=jnp.int32) * tm
    blk_e = jnp.minimum(jnp.searchsorted(pend, blk_start, side='right'), n_experts - 1).astype(jnp.int32)
    blk_first = jnp.concatenate([jnp.ones((1,), jnp.int32), (blk_e[1:] != blk_e[:-1]).astype(jnp.int32)])
    n_used = pend[-1:].astype(jnp.int32)
    return pos, row_tok, blk_e, blk_first, n_used // tm, n_used


def _moe(h, top_e, layer, w_gate, b_gate, w_up, b_up, w_down, b_down):
    E = w_gate.shape[1]
    pos, row_tok, blk_e, blk_first, n_blk, n_used = _route(top_e, E)
    xs = _gather_rows(h, row_tok, n_used)
    a = _moe_grouped(_moe_up_kernel, xs, [w_gate, w_up], [b_gate, b_up], layer, blk_e, blk_first, n_blk,
                     BF16, 256, "moe_up")
    y = _moe_grouped(_moe_down_kernel, a, [w_down], [b_down], layer, blk_e, blk_first, n_blk,
                     F32, 512, "moe_down")
    return y, pos


def kernel(x_prompt, x_sample, c_prompt, c_sample, state_hgrn, cache_k, cache_v, norm_mix, norm_ffn, w_ada, b_ada, w_in_a, lb_param, gnorm_a, w_out_a, kv_norm, w_kv, k_norm, w_q_b, q_norm, rel_bias, w_out_b, w_router, b_router, w_gate, b_gate, w_up, b_up, w_down, b_down):
    Bp, Lp, D = x_prompt.shape
    Bs, Ls, _ = x_sample.shape
    depth = norm_mix.shape[0]
    n_a = w_in_a.shape[0]
    H = D // HEAD_DIM
    Mp, Ms = Bp * Lp, Bs * Ls
    streams = ((Bp, Lp), (Bs, Ls))

    cond = jnp.concatenate([c_prompt, c_sample], axis=0)
    cond = (cond * jax.nn.sigmoid(cond)).astype(BF16)
    mods = []
    for l in range(depth):
        (mod,) = _matmul(_mm_bias_kernel, cond, w_ada[l].astype(BF16), [b_ada[l].reshape(1, -1)],
                         [pl.BlockSpec((1, 1024), lambda i, j: (0, j))], [F32],
                         tm=cond.shape[0], tn=1024, name="ada")
        mods.append(jnp.split(mod, 6, axis=-1))

    def per_stream(arr, s):
        return arr[:Bp] if s == 0 else arr[Bp:]

    def gate_for(arr, s):
        g = per_stream(arr, s)
        if s == 0:
            return g.reshape(Bp, 1, D)
        return jnp.repeat(g, Ls, axis=0)

    lb_all = jnp.cumsum(jax.nn.softmax(lb_param.astype(F32), axis=0), axis=0)
    xs = [x_prompt.reshape(Mp, D), x_sample.reshape(Ms, D)]
    s0s = [jnp.zeros((n_a, Bp, H, HEAD_DIM, HEAD_DIM), state_hgrn.dtype), state_hgrn]
    states = [[], []]
    k32 = [None, None]
    v32 = [None, None]
    k16 = [None, None]
    v16 = [None, None]
    zeros_mod = [jnp.zeros((Bp, D), F32), jnp.zeros((Bs, D), F32)]

    for l in range(depth):
        sh1, sc1, gt1, sh2, sc2, gt2 = mods[l]
        for s, (B, L) in enumerate(streams):
            tm, tn = _mm_tiles(B, L, s == 0)
            h = _norm(xs[s], norm_mix[l], per_stream(sc1, s), per_stream(sh1, s), B, L, BF16)
            if l < n_a:
                (proj,) = _matmul(_mm_plain_kernel, h, w_in_a[l].astype(BF16), [], [], [F32],
                                  tm=tm, tn=tn, name="hgrn_in")
                o, S = _hgrn(proj, lb_all[l], gnorm_a[l], s0s[s][l].astype(F32), B, L)
                states[s].append(S.astype(state_hgrn.dtype))
                w_o = w_out_a[l]
            else:
                jb = l - n_a
                (q,) = _matmul(functools.partial(_mm_headnorm_kernel, scale=HEAD_DIM ** -0.5), h,
                               w_q_b[jb].astype(BF16), [q_norm[jb].reshape(1, HEAD_DIM)],
                               [pl.BlockSpec((1, HEAD_DIM), lambda i, j: (0, 0))], [BF16],
                               tm=tm, tn=tn, name="q_proj")
                bias_tile = _rel_bias_tile(rel_bias[jb])
                if s == 0:
                    o = _attn_prompt(q, k16[s], v16[s], bias_tile, B, L)
                else:
                    o = _attn_sample(q, cache_k.reshape(B, -1, D), cache_v.reshape(B, -1, D),
                                     k16[s], v16[s], bias_tile, B, L)
                w_o = w_out_b[jb]
            g1 = gate_for(gt1, s)
            (xs[s],) = _matmul(_mm_resgate_kernel, o, w_o.astype(BF16), [xs[s], g1],
                               [pl.BlockSpec((tm, tn), lambda i, j: (i, j)), _gate_spec(g1, tm, tn, L)],
                               [F32], tm=tm, tn=tn, name="out_proj")
        hs, gs, es = [], [], []
        for s, (B, L) in enumerate(streams):
            h, gates, top_e = _norm_router(xs[s], norm_ffn[l], per_stream(sc2, s), per_stream(sh2, s),
                                           w_router[l], b_router[l], B, L)
            hs.append(h)
            gs.append(gates)
            es.append(top_e[:, :TOP_K])
        y_sorted, pos = _moe(jnp.concatenate(hs, axis=0), jnp.concatenate(es, axis=0), l,
                             w_gate, b_gate, w_up, b_up, w_down, b_down)
        pos_s = [pos[:Mp * TOP_K], pos[Mp * TOP_K:]]
        for s, (B, L) in enumerate(streams):
            xs[s] = _combine(y_sorted, pos_s[s], gs[s], xs[s], gate_for(gt2, s), L)
        if l == n_a - 1:
            w_k = w_kv[:, :D].astype(BF16)
            w_v = w_kv[:, D:].astype(BF16)
            for s, (B, L) in enumerate(streams):
                tm, tn = _mm_tiles(B, L, s == 0)
                hn = _norm(xs[s], kv_norm, zeros_mod[s], zeros_mod[s], B, L, BF16)
                k32[s], k16[s] = _matmul(functools.partial(_mm_headnorm_kernel, scale=1.0), hn, w_k,
                                         [k_norm.reshape(1, HEAD_DIM)],
                                         [pl.BlockSpec((1, HEAD_DIM), lambda i, j: (0, 0))], [F32, BF16],
                                         tm=tm, tn=tn, name="k_proj")
                v32[s], v16[s] = _matmul(_mm_plain_kernel, hn, w_v, [], [], [F32, BF16],
                                         tm=tm, tn=tn, name="v_proj")

    rows = min(BAND_ROWS, Lp)
    kp = k32[0].reshape(Bp, Lp, H, HEAD_DIM)[:, Lp - rows:]
    vp = v32[0].reshape(Bp, Lp, H, HEAD_DIM)[:, Lp - rows:]
    return (xs[0].reshape(Bp, Lp, D), xs[1].reshape(Bs, Ls, D),
            jnp.stack(states[0]), jnp.stack(states[1]),
            kp, vp,
            k32[1].reshape(Bs, Ls, H, HEAD_DIM), v32[1].reshape(Bs, Ls, H, HEAD_DIM))
```

```python
import functools

import jax
import jax.numpy as jnp
from jax import lax
from jax.experimental import pallas as pl
from jax.experimental.pallas import tpu as pltpu

CHUNK = 64
BAND_ROWS = 8 * CHUNK
REL_CLIP = 256
HEAD_DIM = 128
SUB = 16
HGRN_HEADS = 8
TOP_K = 4
SWIGLU_ALPHA = 1.702
SWIGLU_LIMIT = 7.0
EPS = 1e-6
NEG_INF = -1e30

V7X_LANES = 128
V7X_VMEM_BYTES = 64 * 1024 * 1024
V7X_VMEM_CAP = V7X_VMEM_BYTES - 8 * 1024 * 1024
VMEM_SLACK = 6 * 1024 * 1024

BF16 = jnp.bfloat16
F32 = jnp.float32


def _nbytes(shape, dtype):
    n = 1
    for s in shape:
        n *= s
    return n * jnp.dtype(dtype).itemsize


def _params(semantics, block_bytes, scratch_bytes=0):
    limit = min(2 * block_bytes + scratch_bytes + VMEM_SLACK, V7X_VMEM_CAP)
    return pltpu.CompilerParams(dimension_semantics=semantics, vmem_limit_bytes=int(limit))


def _sigmoid(x):
    return 1.0 / (1.0 + jnp.exp(-x))


def _dot(a, b):
    return jnp.dot(a, b, preferred_element_type=F32)


def _dot_nt(a, b):
    return lax.dot_general(a, b, (((1,), (1,)), ((), ())), preferred_element_type=F32)


def _dot_tn(a, b):
    return lax.dot_general(a, b, (((0,), (0,)), ((), ())), preferred_element_type=F32)


def _mm_bias_kernel(x_ref, w_ref, b_ref, o_ref):
    o_ref[...] = _dot(x_ref[...], w_ref[...].astype(BF16)) + b_ref[...]


def _mm_plain_kernel(x_ref, w_ref, *o_refs):
    y = _dot(x_ref[...], w_ref[...])
    for o_ref in o_refs:
        o_ref[...] = y.astype(o_ref.dtype)


def _mm_resgate_kernel(x_ref, w_ref, res_ref, g_ref, o_ref):
    o_ref[...] = res_ref[...] + g_ref[...] * _dot(x_ref[...], w_ref[...])


def _mm_headnorm_kernel(x_ref, w_ref, gn_ref, *o_refs, scale):
    y = _dot(x_ref[...], w_ref[...])
    gn = gn_ref[...]
    for hh in range(y.shape[1] // HEAD_DIM):
        sl = slice(hh * HEAD_DIM, (hh + 1) * HEAD_DIM)
        seg = y[:, sl]
        ms = jnp.mean(seg * seg, axis=-1, keepdims=True)
        z = seg * lax.rsqrt(ms + EPS) * gn
        if scale != 1.0:
            z = z * scale
        for o_ref in o_refs:
            o_ref[:, sl] = z.astype(o_ref.dtype)


def _matmul(body, x, w, extras, extra_specs, out_dtypes, *, tm, tn, name):
    M, K = x.shape
    N = w.shape[1]
    assert M % tm == 0 and N % tn == 0, (M, N, tm, tn)
    in_specs = [pl.BlockSpec((tm, K), lambda i, j: (i, 0)),
                pl.BlockSpec((K, tn), lambda i, j: (0, j))] + list(extra_specs)
    out_specs = [pl.BlockSpec((tm, tn), lambda i, j: (i, j)) for _ in out_dtypes]
    out_shape = [jax.ShapeDtypeStruct((M, N), dt) for dt in out_dtypes]
    block_bytes = (_nbytes((tm, K), x.dtype) + _nbytes((K, tn), w.dtype)
                   + sum(_nbytes((tm, tn), dt) for dt in out_dtypes)
                   + len(extras) * _nbytes((tm, tn), F32))
    outs = pl.pallas_call(
        body, grid=(M // tm, N // tn), in_specs=in_specs, out_specs=out_specs, out_shape=out_shape,
        compiler_params=_params(("parallel", "arbitrary"), block_bytes), name=name,
    )(x, w, *extras)
    return outs


MM_TM = 1024
MM_TN = 512


def _gate_spec(gate, tm, tn, rows_per_seq):
    if gate.ndim == 3:
        assert rows_per_seq % tm == 0
        per = rows_per_seq // tm
        return pl.BlockSpec((None, 1, tn), lambda i, j: (i // per, 0, j))
    return pl.BlockSpec((tm, tn), lambda i, j: (i, j))


def _mm_tiles(B, L, per_sequence_gate):
    rows = L if per_sequence_gate else B * L
    tm = MM_TM if rows % MM_TM == 0 else rows
    return tm, MM_TN


def _norm_mod(x_ref, g_ref, sc_ref, sh_ref):
    x = x_ref[...]
    ms = jnp.mean(x * x, axis=-1, keepdims=True)
    y = x * lax.rsqrt(ms + EPS) * g_ref[...]
    return y * (1.0 + sc_ref[...]) + sh_ref[...]


def _norm_kernel(x_ref, g_ref, sc_ref, sh_ref, h_ref):
    h_ref[...] = _norm_mod(x_ref, g_ref, sc_ref, sh_ref).astype(h_ref.dtype)


def _norm_router_kernel(x_ref, g_ref, sc_ref, sh_ref, wr_ref, br_ref, h_ref, gate_ref, idx_ref, *, n_experts):
    h = _norm_mod(x_ref, g_ref, sc_ref, sh_ref)
    h_ref[...] = h
    hi = h.astype(BF16)
    lo = (h - hi.astype(F32)).astype(BF16)
    w = wr_ref[...]
    whi = w.astype(BF16)
    wlo = (w - whi.astype(F32)).astype(BF16)
    logits = _dot(hi, whi) + _dot(hi, wlo) + _dot(lo, whi) + br_ref[...]
    lane = lax.broadcasted_iota(jnp.int32, logits.shape, 1)
    neg = jnp.float32(-jnp.inf)
    l = jnp.where(lane < n_experts, logits, neg)
    vals, idxs = [], []
    for _ in range(TOP_K):
        m = jnp.max(l, axis=-1, keepdims=True)
        idx = jnp.min(jnp.where(l == m, lane, V7X_LANES), axis=-1, keepdims=True)
        vals.append(m)
        idxs.append(idx)
        l = jnp.where(lane == idx, neg, l)
    exps = [jnp.exp(v - vals[0]) for v in vals]
    denom = exps[0]
    for e in exps[1:]:
        denom = denom + e
    gates = jnp.zeros(logits.shape, F32)
    sel = jnp.zeros(logits.shape, jnp.int32)
    for k in range(TOP_K):
        gates = jnp.where(lane == k, exps[k] / denom, gates)
        sel = jnp.where(lane == k, idxs[k], sel)
    gate_ref[...] = gates
    idx_ref[...] = sel


def _norm_rows(L):
    return 256 if L % 256 == 0 else L


def _norm_specs(B, L, D, tl):
    nl = L // tl
    x_spec = pl.BlockSpec((tl, D), lambda b, l: (b * nl + l, 0))
    g_spec = pl.BlockSpec((1, D), lambda b, l: (0, 0))
    m_spec = pl.BlockSpec((None, 1, D), lambda b, l: (b, 0, 0))
    return x_spec, g_spec, m_spec


def _norm(x, g, sc, sh, B, L, out_dtype):
    D = x.shape[1]
    tl = _norm_rows(L)
    x_spec, g_spec, m_spec = _norm_specs(B, L, D, tl)
    block_bytes = _nbytes((tl, D), F32) + _nbytes((tl, D), out_dtype) + 3 * _nbytes((1, D), F32)
    return pl.pallas_call(
        _norm_kernel, grid=(B, L // tl),
        in_specs=[x_spec, g_spec, m_spec, m_spec], out_specs=x_spec,
        out_shape=jax.ShapeDtypeStruct(x.shape, out_dtype),
        compiler_params=_params(("parallel", "parallel"), block_bytes), name="norm_mod",
    )(x, g.reshape(1, D), sc.reshape(B, 1, D), sh.reshape(B, 1, D))


def _norm_router(x, g, sc, sh, w_router, b_router, B, L):
    M, D = x.shape
    E = w_router.shape[1]
    tl = _norm_rows(L)
    x_spec, g_spec, m_spec = _norm_specs(B, L, D, tl)
    wr = jnp.pad(w_router, ((0, 0), (0, V7X_LANES - E)))
    br = jnp.pad(b_router, (0, V7X_LANES - E)).reshape(1, V7X_LANES)
    r_spec = pl.BlockSpec((tl, V7X_LANES), lambda b, l: (b * (L // tl) + l, 0))
    block_bytes = (2 * _nbytes((tl, D), F32) + 3 * _nbytes((1, D), F32) + _nbytes((D, V7X_LANES), F32)
                   + 2 * _nbytes((tl, V7X_LANES), F32))
    return pl.pallas_call(
        functools.partial(_norm_router_kernel, n_experts=E), grid=(B, L // tl),
        in_specs=[x_spec, g_spec, m_spec, m_spec,
                  pl.BlockSpec((D, V7X_LANES), lambda b, l: (0, 0)),
                  pl.BlockSpec((1, V7X_LANES), lambda b, l: (0, 0))],
        out_specs=[x_spec, r_spec, r_spec],
        out_shape=[jax.ShapeDtypeStruct((M, D), F32),
                   jax.ShapeDtypeStruct((M, V7X_LANES), F32),
                   jax.ShapeDtypeStruct((M, V7X_LANES), jnp.int32)],
        compiler_params=_params(("parallel", "parallel"), block_bytes), name="norm_router",
    )(x, g.reshape(1, D), sc.reshape(B, 1, D), sh.reshape(B, 1, D), wr, br)


def _hgrn_kernel(q_ref, f_ref, i_ref, og_ref, lb_ref, gn_ref, s0_ref, o_ref, sout_ref, s_scr, *, C, Hg):
    c = pl.program_id(2)
    nsub = C // SUB
    W = Hg * HEAD_DIM

    @pl.when(c == 0)
    def _():
        s_scr[...] = s0_ref[...]

    lb = lb_ref[...]
    f = lb + (1.0 - lb) * _sigmoid(f_ref[...])
    logf = jnp.log(f)
    kk = 1.0 - f
    row = lax.broadcasted_iota(jnp.int32, (C, C), 0)
    col = lax.broadcasted_iota(jnp.int32, (C, C), 1)
    tri = jnp.where(col <= row, 1.0, 0.0).astype(BF16)
    a1 = logf.astype(BF16)
    r1 = logf - a1.astype(F32)
    a2 = r1.astype(BF16)
    a3 = (r1 - a2.astype(F32)).astype(BF16)
    G = _dot(tri, a1) + _dot(tri, a2) + _dot(tri, a3)

    q = q_ref[...] * (HEAD_DIM ** -0.5)
    v = i_ref[...].astype(BF16)
    og = og_ref[...]
    gn = gn_ref[...]
    ends = [G[SUB * j + SUB - 1:SUB * j + SUB, :] for j in range(nsub)]
    bblk = jnp.concatenate([jnp.broadcast_to(e, (SUB, W)) for e in ends], axis=0)
    kt = kk * jnp.exp(bblk - G)
    k_end = (kt * jnp.exp(ends[-1] - bblk)).astype(BF16)
    kt = kt.astype(BF16)
    q_in = (q * jnp.exp(G)).astype(BF16)
    decay = jnp.exp(ends[-1])

    heads = [slice(h * HEAD_DIM, (h + 1) * HEAD_DIM) for h in range(Hg)]
    scores = []
    for sl in heads:
        row = []
        for j in range(nsub):
            r0 = SUB * j
            qj = (q[r0:, sl] * jnp.exp(G[r0:, sl] - ends[j][:, sl])).astype(BF16)
            row.append(_dot_nt(qj, kt[r0:r0 + SUB, sl]))
        scores.append(row)
    states = [s_scr[h] for h in range(Hg)]
    outs = [_dot(q_in[:, sl], states[h].astype(BF16)) for h, sl in enumerate(heads)]
    kvs = [_dot_tn(k_end[:, sl], v[:, sl]) for sl in heads]
    for h, sl in enumerate(heads):
        for j in range(nsub):
            r0 = SUB * j
            a = scores[h][j]
            rr = lax.broadcasted_iota(jnp.int32, a.shape, 0)
            ss = lax.broadcasted_iota(jnp.int32, a.shape, 1)
            a = jnp.where(rr >= ss, a, 0.0)
            part = _dot(a.astype(BF16), v[r0:r0 + SUB, sl])
            if r0:
                part = jnp.concatenate([jnp.zeros((r0, HEAD_DIM), F32), part], axis=0)
            outs[h] = outs[h] + part
    for h, sl in enumerate(heads):
        dcol = jnp.broadcast_to(decay[:, sl], (HEAD_DIM, HEAD_DIM)).T
        S_new = states[h] * dcol + kvs[h]
        s_scr[h] = S_new
        sout_ref[h] = S_new
        o = outs[h]
        ms = jnp.mean(o * o, axis=-1, keepdims=True)
        ogh = og[:, sl]
        y = o * lax.rsqrt(ms + EPS) * gn * (ogh * _sigmoid(ogh))
        o_ref[:, sl] = y.astype(o_ref.dtype)


def _hgrn(proj, lb, gnorm, s0, B, L):
    M, D4 = proj.shape
    D = D4 // 4
    H = D // HEAD_DIM
    C = CHUNK if L % CHUNK == 0 else L
    assert L % C == 0 and C % SUB == 0
    Hg = HGRN_HEADS
    W = Hg * HEAD_DIM
    nc, ng = L // C, H // Hg

    def sec_spec(sec):
        return pl.BlockSpec((C, W), lambda b, g, c: (b * nc + c, sec * ng + g))

    s_spec = pl.BlockSpec((None, Hg, HEAD_DIM, HEAD_DIM), lambda b, g, c: (b, g, 0, 0))
    block_bytes = 4 * _nbytes((C, W), F32) + _nbytes((C, W), BF16) + 2 * _nbytes((Hg, HEAD_DIM, HEAD_DIM), F32)
    o, s_out = pl.pallas_call(
        functools.partial(_hgrn_kernel, C=C, Hg=Hg), grid=(B, ng, nc),
        in_specs=[sec_spec(0), sec_spec(1), sec_spec(2), sec_spec(3),
                  pl.BlockSpec((1, W), lambda b, g, c: (0, g)),
                  pl.BlockSpec((1, HEAD_DIM), lambda b, g, c: (0, 0)),
                  s_spec],
        out_specs=[pl.BlockSpec((C, W), lambda b, g, c: (b * nc + c, g)), s_spec],
        out_shape=[jax.ShapeDtypeStruct((M, D), BF16),
                   jax.ShapeDtypeStruct((B, H, HEAD_DIM, HEAD_DIM), F32)],
        scratch_shapes=[pltpu.VMEM((Hg, HEAD_DIM, HEAD_DIM), F32)],
        compiler_params=_params(("parallel", "parallel", "arbitrary"), block_bytes,
                                _nbytes((Hg, HEAD_DIM, HEAD_DIM), F32)),
        name="hgrn2",
    )(proj, proj, proj, proj, lb.reshape(1, D), gnorm.reshape(1, HEAD_DIM), s0)
    return o, s_out


QT = 2 * CHUNK
BAND_T = BAND_ROWS + QT
ATTN_UNROLL = 4


def _attn_prompt_kernel(q_ref, k_ref, v_ref, bias_ref, o_ref, kpad, vpad):
    L = q_ref.shape[0]
    kpad[0:BAND_ROWS, :] = jnp.zeros((BAND_ROWS, HEAD_DIM), BF16)
    vpad[0:BAND_ROWS, :] = jnp.zeros((BAND_ROWS, HEAD_DIM), BF16)
    kpad[BAND_ROWS:, :] = k_ref[...]
    vpad[BAND_ROWS:, :] = v_ref[...]
    t = lax.broadcasted_iota(jnp.int32, (QT, BAND_T), 0)
    j = lax.broadcasted_iota(jnp.int32, (QT, BAND_T), 1)
    first = jnp.where(t >= CHUNK, CHUNK, 0)
    in_band = (j >= first) & (j < first + BAND_ROWS + CHUNK)
    bias = jnp.where(in_band, bias_ref[...], NEG_INF)

    def tile(i, carry):
        start = pl.multiple_of(i * QT, QT)
        q = q_ref[pl.ds(start, QT), :]
        kb = kpad[pl.ds(start, BAND_T), :]
        vb = vpad[pl.ds(start, BAND_T), :]
        s = _dot_nt(q, kb) + bias
        s = jnp.where(j >= BAND_ROWS - i * QT, s, NEG_INF)
        m = jnp.max(s, axis=-1, keepdims=True)
        p = jnp.exp(s - m)
        l = jnp.sum(p, axis=-1, keepdims=True)
        o = _dot(p.astype(BF16), vb) / l
        o_ref[pl.ds(start, QT), :] = o.astype(o_ref.dtype)
        return carry

    n_tiles = L // QT
    unroll = ATTN_UNROLL if n_tiles % ATTN_UNROLL == 0 else 1
    lax.fori_loop(0, n_tiles, tile, 0, unroll=unroll)


def _attn_prompt(q, k, v, bias, B, L):
    M, D = q.shape
    H = D // HEAD_DIM
    assert L % QT == 0
    spec = pl.BlockSpec((L, HEAD_DIM), lambda b, h: (b, h))
    block_bytes = 4 * _nbytes((L, HEAD_DIM), BF16) + _nbytes((QT, BAND_T), F32)
    scratch = 2 * _nbytes((L + BAND_ROWS, HEAD_DIM), BF16)
    return pl.pallas_call(
        _attn_prompt_kernel, grid=(B, H),
        in_specs=[spec, spec, spec, pl.BlockSpec((None, QT, BAND_T), lambda b, h: (h, 0, 0))],
        out_specs=spec, out_shape=jax.ShapeDtypeStruct((M, D), BF16),
        scratch_shapes=[pltpu.VMEM((L + BAND_ROWS, HEAD_DIM), BF16)] * 2,
        compiler_params=_params(("parallel", "parallel"), block_bytes, scratch), name="attn_prompt",
    )(q, k, v, bias)


def _attn_sample_kernel(q_ref, ck_ref, cv_ref, kn_ref, vn_ref, bias_ref, o_ref, *, Hg):
    T = q_ref.shape[0]
    J0 = ck_ref.shape[0]
    for h in range(Hg):
        sl = slice(h * HEAD_DIM, (h + 1) * HEAD_DIM)
        q = q_ref[:, sl]
        s1 = _dot_nt(q, ck_ref[:, h, :].astype(BF16)) + bias_ref[h, 0:T, 0:J0]
        s2 = _dot_nt(q, kn_ref[:, sl]) + bias_ref[h, 0:T, J0:J0 + T]
        m = jnp.maximum(jnp.max(s1, axis=-1, keepdims=True), jnp.max(s2, axis=-1, keepdims=True))
        p1 = jnp.exp(s1 - m)
        p2 = jnp.exp(s2 - m)
        l = jnp.sum(p1, axis=-1, keepdims=True) + jnp.sum(p2, axis=-1, keepdims=True)
        o = _dot(p1.astype(BF16), cv_ref[:, h, :].astype(BF16)) + _dot(p2.astype(BF16), vn_ref[:, sl])
        o_ref[:, sl] = (o / l).astype(o_ref.dtype)


def _attn_sample(q, cache_k, cache_v, k_new, v_new, bias, B, T):
    M, D = q.shape
    H = D // HEAD_DIM
    J0 = cache_k.shape[1]
    assert J0 == BAND_ROWS and T <= QT and J0 + T <= BAND_T
    Hg = 8
    W = Hg * HEAD_DIM
    n_spec = pl.BlockSpec((T, W), lambda b, g: (b, g))
    c_spec = pl.BlockSpec((None, J0, Hg, HEAD_DIM), lambda b, g: (b, 0, g, 0))
    block_bytes = (3 * _nbytes((T, W), BF16) + 2 * _nbytes((J0, W), F32) + _nbytes((Hg, QT, BAND_T), F32)
                   + _nbytes((T, W), BF16))
    return pl.pallas_call(
        functools.partial(_attn_sample_kernel, Hg=Hg), grid=(B, H // Hg),
        in_specs=[n_spec, c_spec, c_spec, n_spec, n_spec,
                  pl.BlockSpec((Hg, QT, BAND_T), lambda b, g: (g, 0, 0))],
        out_specs=n_spec, out_shape=jax.ShapeDtypeStruct((M, D), BF16),
        compiler_params=_params(("parallel", "parallel"), block_bytes), name="attn_sample",
    )(q, cache_k, cache_v, k_new, v_new, bias)


def _rel_bias_tile(table):
    rel = jnp.arange(QT)[:, None] + BAND_ROWS - jnp.arange(BAND_T)[None, :]
    idx = jnp.clip(rel, -(CHUNK - 1), REL_CLIP) + (CHUNK - 1)
    return table[:, idx].astype(F32)


MOE_TM = 512
MOE_UP_TN = 512
MOE_DOWN_TN = 1024
ROUTE_CHUNK = 128
CAST_ROWS = 256
GATHER_ROWS = 256
COMBINE_ROWS = 128
ROW_DMA_UNROLL = 8


def _gather_kernel(idx_ref, nrows_ref, h_hbm, o_ref, buf, sem):
    R = o_ref.shape[0]
    i = pl.program_id(0)
    slot = lax.rem(i, 2)

    def issue(blk, slot_):
        def start(r, c):
            src = idx_ref[blk * R + r]
            pltpu.make_async_copy(h_hbm.at[pl.ds(src, 1)], buf.at[slot_, pl.ds(r, 1)], sem.at[slot_]).start()
            return c

        lax.fori_loop(0, R, start, 0, unroll=ROW_DMA_UNROLL)

    @pl.when(i == 0)
    def _():
        issue(0, 0)

    @pl.when((i + 1 < pl.num_programs(0)) & ((i + 1) * R < nrows_ref[0]))
    def _():
        issue(i + 1, 1 - slot)

    @pl.when(i * R < nrows_ref[0])
    def _():
        def wait(r, c):
            pltpu.make_async_copy(h_hbm.at[pl.ds(0, 1)], buf.at[slot, pl.ds(r, 1)], sem.at[slot]).wait()
            return c

        lax.fori_loop(0, R, wait, 0, unroll=ROW_DMA_UNROLL)
        o_ref[...] = buf[slot].astype(o_ref.dtype)

    @pl.when(i * R >= nrows_ref[0])
    def _():
        o_ref[...] = jnp.zeros(o_ref.shape, o_ref.dtype)


def _gather_rows(h, row_tok, n_used_rows):
    n_rows = row_tok.shape[0]
    D = h.shape[1]
    R = GATHER_ROWS
    assert n_rows % R == 0
    grid_spec = pltpu.PrefetchScalarGridSpec(
        num_scalar_prefetch=2, grid=(n_rows // R,),
        in_specs=[pl.BlockSpec(memory_space=pl.ANY)],
        out_specs=pl.BlockSpec((R, D), lambda i, idx, n: (i, 0)),
        scratch_shapes=[pltpu.VMEM((2, R, D), F32), pltpu.SemaphoreType.DMA((2,))])
    return pl.pallas_call(
        _gather_kernel, grid_spec=grid_spec, out_shape=jax.ShapeDtypeStruct((n_rows, D), BF16),
        compiler_params=_params(("arbitrary",), _nbytes((R, D), BF16), _nbytes((2, R, D), F32)),
        name="moe_gather",
    )(row_tok, n_used_rows, h)


def _refresh_weights(be_ref, nxt_ref, w_hbms, stage, wbf, sem, *, layer, tn):
    j = pl.program_id(0)
    i = pl.program_id(1)

    def tile_copy(k, e, jj):
        col = pl.multiple_of(jj * tn, tn)
        return pltpu.make_async_copy(w_hbms[k].at[layer, e, :, pl.ds(col, tn)], stage.at[k], sem.at[k])

    @pl.when((j == 0) & (i == 0))
    def _():
        for k in range(len(w_hbms)):
            tile_copy(k, be_ref[0], 0).start()

    nxt = nxt_ref[i]
    for k in range(len(w_hbms)):
        tile_copy(k, be_ref[i], j).wait()

        def cast_rows(c, carry):
            rows = pl.ds(pl.multiple_of(c * CAST_ROWS, CAST_ROWS), CAST_ROWS)
            wbf[k, rows, :] = stage[k, rows, :].astype(BF16)
            return carry

        lax.fori_loop(0, stage.shape[1] // CAST_ROWS, cast_rows, 0)

        @pl.when(nxt >= 0)
        def _():
            tile_copy(k, nxt, j).start()

        @pl.when((nxt < 0) & (j + 1 < pl.num_programs(0)))
        def _():
            tile_copy(k, be_ref[0], j + 1).start()


def _moe_up_kernel(be_ref, first_ref, nxt_ref, nblk_ref, x_ref, bg_ref, bu_ref, wg_hbm, wu_hbm, a_ref,
                   stage, wbf, sem, *, layer, tn):
    i = pl.program_id(1)

    @pl.when(i < nblk_ref[0])
    def _():
        @pl.when(first_ref[i] == 1)
        def _():
            _refresh_weights(be_ref, nxt_ref, (wg_hbm, wu_hbm), stage, wbf, sem, layer=layer, tn=tn)

        x = x_ref[...]
        g = jnp.minimum(_dot(x, wbf[0]) + bg_ref[...], SWIGLU_LIMIT)
        u = jnp.clip(_dot(x, wbf[1]) + bu_ref[...], -SWIGLU_LIMIT, SWIGLU_LIMIT)
        a_ref[...] = (g * _sigmoid(SWIGLU_ALPHA * g) * (u + 1.0)).astype(a_ref.dtype)

    @pl.when(i >= nblk_ref[0])
    def _():
        a_ref[...] = jnp.zeros(a_ref.shape, a_ref.dtype)


def _moe_down_kernel(be_ref, first_ref, nxt_ref, nblk_ref, a_ref, bd_ref, wd_hbm, y_ref, stage, wbf, sem,
                     *, layer, tn):
    i = pl.program_id(1)

    @pl.when(i < nblk_ref[0])
    def _():
        @pl.when(first_ref[i] == 1)
        def _():
            _refresh_weights(be_ref, nxt_ref, (wd_hbm,), stage, wbf, sem, layer=layer, tn=tn)

        y_ref[...] = _dot(a_ref[...], wbf[0]) + bd_ref[...]

    @pl.when(i >= nblk_ref[0])
    def _():
        y_ref[...] = jnp.zeros(y_ref.shape, y_ref.dtype)


def _moe_grouped(body, x, weights, biases, layer, blk_e, blk_first, blk_next, n_blk, out_dtype, tn, name):
    n_rows, K = x.shape
    N = weights[0].shape[-1]
    tm = MOE_TM
    n_blocks = n_rows // tm
    nw = len(weights)

    def x_map(j, i, be, fi, nx, nb):
        return (jnp.minimum(i, nb[0] - 1), 0)

    def b_map(j, i, be, fi, nx, nb):
        return (layer, be[i], 0, j)

    in_specs = [pl.BlockSpec((tm, K), x_map)]
    in_specs += [pl.BlockSpec((None, None, 1, tn), b_map) for _ in biases]
    in_specs += [pl.BlockSpec(memory_space=pl.ANY) for _ in weights]
    grid_spec = pltpu.PrefetchScalarGridSpec(
        num_scalar_prefetch=4, grid=(N // tn, n_blocks), in_specs=in_specs,
        out_specs=pl.BlockSpec((tm, tn), lambda j, i, be, fi, nx, nb: (i, j)),
        scratch_shapes=[pltpu.VMEM((nw, K, tn), F32), pltpu.VMEM((nw, K, tn), BF16),
                        pltpu.SemaphoreType.DMA((nw,))])
    block_bytes = _nbytes((tm, K), x.dtype) + _nbytes((tm, tn), out_dtype) + len(biases) * _nbytes((8, tn), F32)
    scratch = nw * (_nbytes((K, tn), F32) + _nbytes((K, tn), BF16)) + (nw + 2) * _nbytes((tm, tn), F32)
    b4 = [b.reshape(b.shape[0], b.shape[1], 1, b.shape[2]) for b in biases]
    return pl.pallas_call(
        functools.partial(body, layer=layer, tn=tn), grid_spec=grid_spec,
        out_shape=jax.ShapeDtypeStruct((n_rows, N), out_dtype),
        compiler_params=_params(("arbitrary", "arbitrary"), block_bytes, scratch), name=name,
    )(blk_e, blk_first, blk_next, n_blk, x, *b4, *weights)


def _combine_kernel(pos_ref, y_hbm, gates_ref, x_ref, gt_ref, o_ref, buf, sem):
    R = x_ref.shape[0]
    i = pl.program_id(0)
    slot = lax.rem(i, 2)

    def row_copy(slot_, k, r, src_row):
        return pltpu.make_async_copy(y_hbm.at[pl.ds(src_row, 1)], buf.at[slot_, k, pl.ds(r, 1)], sem.at[slot_])

    def issue(blk, slot_):
        def start(r, c):
            for k in range(TOP_K):
                row_copy(slot_, k, r, pos_ref[(blk * R + r) * TOP_K + k]).start()
            return c

        lax.fori_loop(0, R, start, 0, unroll=ROW_DMA_UNROLL // TOP_K)

    @pl.when(i == 0)
    def _():
        issue(0, 0)

    @pl.when(i + 1 < pl.num_programs(0))
    def _():
        issue(i + 1, 1 - slot)

    def wait(r, c):
        for k in range(TOP_K):
            row_copy(slot, k, r, 0).wait()
        return c

    lax.fori_loop(0, R, wait, 0, unroll=ROW_DMA_UNROLL // TOP_K)
    gates = gates_ref[...]
    moe = gates[:, 0:1] * buf[slot, 0]
    for k in range(1, TOP_K):
        moe = moe + gates[:, k:k + 1] * buf[slot, k]
    o_ref[...] = x_ref[...] + gt_ref[...] * moe


def _combine(y_sorted, pos, gates, x, gt, rows_per_seq):
    T, D = x.shape
    R = COMBINE_ROWS if T % COMBINE_ROWS == 0 else T
    if gt.ndim == 3:
        assert rows_per_seq % R == 0
        per = rows_per_seq // R
        gt_spec = pl.BlockSpec((None, 1, D), lambda i, p: (i // per, 0, 0))
    else:
        gt_spec = pl.BlockSpec((R, D), lambda i, p: (i, 0))
    row_spec = pl.BlockSpec((R, D), lambda i, p: (i, 0))
    grid_spec = pltpu.PrefetchScalarGridSpec(
        num_scalar_prefetch=1, grid=(T // R,),
        in_specs=[pl.BlockSpec(memory_space=pl.ANY),
                  pl.BlockSpec((R, V7X_LANES), lambda i, p: (i, 0)), row_spec, gt_spec],
        out_specs=row_spec,
        scratch_shapes=[pltpu.VMEM((2, TOP_K, R, D), F32), pltpu.SemaphoreType.DMA((2,))])
    block_bytes = 3 * _nbytes((R, D), F32) + _nbytes((R, V7X_LANES), F32)
    return pl.pallas_call(
        _combine_kernel, grid_spec=grid_spec, out_shape=jax.ShapeDtypeStruct((T, D), F32),
        compiler_params=_params(("arbitrary",), block_bytes, _nbytes((2, TOP_K, R, D), F32)),
        name="moe_combine",
    )(pos, y_sorted, gates, x, gt)


def _route(top_e, n_experts):
    A = top_e.shape[0] * TOP_K
    tm = MOE_TM
    e_flat = top_e.reshape(A)
    assert A % ROUTE_CHUNK == 0 and A < 2 ** 24
    onehot = (e_flat[:, None] == jnp.arange(n_experts, dtype=jnp.int32)[None, :]).astype(F32)
    oh = onehot.reshape(A // ROUTE_CHUNK, ROUTE_CHUNK, n_experts)
    tri = jnp.tril(jnp.ones((ROUTE_CHUNK, ROUTE_CHUNK), F32))
    local = jnp.einsum('ts,csk->ctk', tri, oh, preferred_element_type=F32)
    tot = local[:, -1, :]
    offs = jnp.cumsum(tot, axis=0) - tot
    counts = (offs[-1] + tot[-1]).astype(jnp.int32)
    rank = (jnp.sum(oh * (local + offs[:, None, :]), axis=-1).reshape(A) - 1.0).astype(jnp.int32)
    padded = (counts + tm - 1) // tm * tm
    pend = jnp.cumsum(padded)
    pstart = pend - padded
    pos = jnp.sum(onehot.astype(jnp.int32) * pstart[None, :], axis=-1) + rank
    n_blocks = -(-A // tm) + n_experts
    n_blocks = -(-n_blocks * tm // GATHER_ROWS) * GATHER_ROWS // tm
    n_rows = n_blocks * tm
    row_tok = jnp.zeros((n_rows,), jnp.int32).at[pos].set(jnp.arange(A, dtype=jnp.int32) // TOP_K)
    blk_start = jnp.arange(n_blocks, dtype=jnp.int32) * tm
    blk_e = jnp.minimum(jnp.searchsorted(pend, blk_start, side='right'), n_experts - 1).astype(jnp.int32)
    blk_first = jnp.concatenate([jnp.ones((1,), jnp.int32), (blk_e[1:] != blk_e[:-1]).astype(jnp.int32)])
    n_used = pend[-1:].astype(jnp.int32)
    n_blk = n_used // tm
    blk_id = jnp.arange(n_blocks, dtype=jnp.int32)
    run_start = jnp.where((blk_first == 1) & (blk_id < n_blk[0]), blk_id, n_blocks)
    later = lax.cummin(run_start, axis=0, reverse=True)
    nxt_id = jnp.concatenate([later[1:], jnp.full((1,), n_blocks, jnp.int32)])
    blk_next = jnp.where(nxt_id < n_blocks, blk_e[jnp.minimum(nxt_id, n_blocks - 1)], -1).astype(jnp.int32)
    return pos, row_tok, blk_e, blk_first, blk_next, n_blk, n_used


def _moe(h, top_e, layer, w_gate, b_gate, w_up, b_up, w_down, b_down):
    E = w_gate.shape[1]
    pos, row_tok, blk_e, blk_first, blk_next, n_blk, n_used = _route(top_e, E)
    xs = _gather_rows(h, row_tok, n_used)
    a = _moe_grouped(_moe_up_kernel, xs, [w_gate, w_up], [b_gate, b_up], layer, blk_e, blk_first, blk_next,
                     n_blk, BF16, MOE_UP_TN, "moe_up")
    y = _moe_grouped(_moe_down_kernel, a, [w_down], [b_down], layer, blk_e, blk_first, blk_next,
                     n_blk, F32, MOE_DOWN_TN, "moe_down")
    return y, pos


def kernel(x_prompt, x_sample, c_prompt, c_sample, state_hgrn, cache_k, cache_v, norm_mix, norm_ffn, w_ada, b_ada, w_in_a, lb_param, gnorm_a, w_out_a, kv_norm, w_kv, k_norm, w_q_b, q_norm, rel_bias, w_out_b, w_router, b_router, w_gate, b_gate, w_up, b_up, w_down, b_down):
    Bp, Lp, D = x_prompt.shape
    Bs, Ls, _ = x_sample.shape
    depth = norm_mix.shape[0]
    n_a = w_in_a.shape[0]
    H = D // HEAD_DIM
    Mp, Ms = Bp * Lp, Bs * Ls
    streams = ((Bp, Lp), (Bs, Ls))

    cond = jnp.concatenate([c_prompt, c_sample], axis=0)
    cond = (cond * jax.nn.sigmoid(cond)).astype(BF16)
    n_cond, n_mod = cond.shape[0], w_ada.shape[2]
    mod = pl.pallas_call(
        _mm_bias_kernel, grid=(depth, n_mod // MM_TN),
        in_specs=[pl.BlockSpec((n_cond, D), lambda l, j: (0, 0)),
                  pl.BlockSpec((None, D, MM_TN), lambda l, j: (l, 0, j)),
                  pl.BlockSpec((None, 1, MM_TN), lambda l, j: (l, 0, j))],
        out_specs=pl.BlockSpec((None, n_cond, MM_TN), lambda l, j: (l, 0, j)),
        out_shape=jax.ShapeDtypeStruct((depth, n_cond, n_mod), F32),
        compiler_params=_params(("parallel", "arbitrary"),
                                _nbytes((D, MM_TN), F32) + _nbytes((n_cond, D), BF16) + _nbytes((n_cond, MM_TN), F32)),
        name="ada",
    )(cond, w_ada, b_ada.reshape(depth, 1, n_mod))
    mods = [jnp.split(mod[l], 6, axis=-1) for l in range(depth)]

    def per_stream(arr, s):
        return arr[:Bp] if s == 0 else arr[Bp:]

    def gate_for(arr, s):
        g = per_stream(arr, s)
        if s == 0:
            return g.reshape(Bp, 1, D)
        return jnp.repeat(g, Ls, axis=0)

    lb_all = jnp.cumsum(jax.nn.softmax(lb_param.astype(F32), axis=0), axis=0)
    xs = [x_prompt.reshape(Mp, D), x_sample.reshape(Ms, D)]
    s0s = [jnp.zeros((n_a, Bp, H, HEAD_DIM, HEAD_DIM), state_hgrn.dtype), state_hgrn]
    states = [[], []]
    k32 = [None, None]
    v32 = [None, None]
    k16 = [None, None]
    v16 = [None, None]
    zeros_mod = [jnp.zeros((Bp, D), F32), jnp.zeros((Bs, D), F32)]

    for l in range(depth):
        sh1, sc1, gt1, sh2, sc2, gt2 = mods[l]
        for s, (B, L) in enumerate(streams):
            tm, tn = _mm_tiles(B, L, s == 0)
            h = _norm(xs[s], norm_mix[l], per_stream(sc1, s), per_stream(sh1, s), B, L, BF16)
            if l < n_a:
                (proj,) = _matmul(_mm_plain_kernel, h, w_in_a[l].astype(BF16), [], [], [F32],
                                  tm=tm, tn=tn, name="hgrn_in")
                o, S = _hgrn(proj, lb_all[l], gnorm_a[l], s0s[s][l].astype(F32), B, L)
                states[s].append(S.astype(state_hgrn.dtype))
                w_o = w_out_a[l]
            else:
                jb = l - n_a
                (q,) = _matmul(functools.partial(_mm_headnorm_kernel, scale=HEAD_DIM ** -0.5), h,
                               w_q_b[jb].astype(BF16), [q_norm[jb].reshape(1, HEAD_DIM)],
                               [pl.BlockSpec((1, HEAD_DIM), lambda i, j: (0, 0))], [BF16],
                               tm=tm, tn=tn, name="q_proj")
                bias_tile = _rel_bias_tile(rel_bias[jb])
                if s == 0:
                    o = _attn_prompt(q, k16[s], v16[s], bias_tile, B, L)
                else:
                    o = _attn_sample(q, cache_k, cache_v, k16[s], v16[s], bias_tile, B, L)
                w_o = w_out_b[jb]
            g1 = gate_for(gt1, s)
            (xs[s],) = _matmul(_mm_resgate_kernel, o, w_o.astype(BF16), [xs[s], g1],
                               [pl.BlockSpec((tm, tn), lambda i, j: (i, j)), _gate_spec(g1, tm, tn, L)],
                               [F32], tm=tm, tn=tn, name="out_proj")
        hs, gs, es = [], [], []
        for s, (B, L) in enumerate(streams):
            h, gates, top_e = _norm_router(xs[s], norm_ffn[l], per_stream(sc2, s), per_stream(sh2, s),
                                           w_router[l], b_router[l], B, L)
            hs.append(h)
            gs.append(gates)
            es.append(top_e[:, :TOP_K])
        y_sorted, pos = _moe(jnp.concatenate(hs, axis=0), jnp.concatenate(es, axis=0), l,
                             w_gate, b_gate, w_up, b_up, w_down, b_down)
        pos_s = [pos[:Mp * TOP_K], pos[Mp * TOP_K:]]
        for s, (B, L) in enumerate(streams):
            xs[s] = _combine(y_sorted, pos_s[s], gs[s], xs[s], gate_for(gt2, s), L)
        if l == n_a - 1:
            w_k = w_kv[:, :D].astype(BF16)
            w_v = w_kv[:, D:].astype(BF16)
            for s, (B, L) in enumerate(streams):
                tm, tn = _mm_tiles(B, L, s == 0)
                hn = _norm(xs[s], kv_norm, zeros_mod[s], zeros_mod[s], B, L, BF16)
                k32[s], k16[s] = _matmul(functools.partial(_mm_headnorm_kernel, scale=1.0), hn, w_k,
                                         [k_norm.reshape(1, HEAD_DIM)],
                                         [pl.BlockSpec((1, HEAD_DIM), lambda i, j: (0, 0))], [F32, BF16],
                                         tm=tm, tn=tn, name="k_proj")
                v32[s], v16[s] = _matmul(_mm_plain_kernel, hn, w_v, [], [], [F32, BF16],
                                         tm=tm, tn=tn, name="v_proj")

    rows = min(BAND_ROWS, Lp)
    kp = k32[0].reshape(Bp, Lp, D)[:, Lp - rows:].reshape(Bp, rows, H, HEAD_DIM)
    vp = v32[0].reshape(Bp, Lp, D)[:, Lp - rows:].reshape(Bp, rows, H, HEAD_DIM)
    return (xs[0].reshape(Bp, Lp, D), xs[1].reshape(Bs, Ls, D),
            jnp.stack(states[0]), jnp.stack(states[1]),
            kp, vp,
            k32[1].reshape(Bs, Ls, H, HEAD_DIM), v32[1].reshape(Bs, Ls, H, HEAD_DIM))
```

```python
import functools

import jax
import jax.numpy as jnp
from jax import lax
from jax.experimental import pallas as pl
from jax.experimental.pallas import tpu as pltpu

CHUNK = 64
BAND_ROWS = 8 * CHUNK
REL_CLIP = 256
HEAD_DIM = 128
SUB = 16
HGRN_HEADS = 8
TOP_K = 4
SWIGLU_ALPHA = 1.702
SWIGLU_LIMIT = 7.0
EPS = 1e-6
NEG_INF = -1e30

V7X_LANES = 128
V7X_VMEM_BYTES = 64 * 1024 * 1024
V7X_VMEM_CAP = V7X_VMEM_BYTES - 8 * 1024 * 1024
VMEM_SLACK = 6 * 1024 * 1024

BF16 = jnp.bfloat16
F32 = jnp.float32


def _nbytes(shape, dtype):
    n = 1
    for s in shape:
        n *= s
    return n * jnp.dtype(dtype).itemsize


def _params(semantics, block_bytes, scratch_bytes=0):
    limit = min(2 * block_bytes + scratch_bytes + VMEM_SLACK, V7X_VMEM_CAP)
    return pltpu.CompilerParams(dimension_semantics=semantics, vmem_limit_bytes=int(limit))


def _sigmoid(x):
    return 1.0 / (1.0 + jnp.exp(-x))


def _dot(a, b):
    return jnp.dot(a, b, preferred_element_type=F32)


def _dot_nt(a, b):
    return lax.dot_general(a, b, (((1,), (1,)), ((), ())), preferred_element_type=F32)


def _dot_tn(a, b):
    return lax.dot_general(a, b, (((0,), (0,)), ((), ())), preferred_element_type=F32)


def _mm_bias_kernel(x_ref, w_ref, b_ref, o_ref):
    o_ref[...] = _dot(x_ref[...], w_ref[...].astype(BF16)) + b_ref[...]


def _mm_plain_kernel(x_ref, w_ref, *o_refs):
    y = _dot(x_ref[...], w_ref[...])
    for o_ref in o_refs:
        o_ref[...] = y.astype(o_ref.dtype)


def _mm_resgate_kernel(x_ref, w_ref, res_ref, g_ref, o_ref):
    o_ref[...] = res_ref[...] + g_ref[...] * _dot(x_ref[...], w_ref[...])


def _mm_headnorm_kernel(x_ref, w_ref, gn_ref, *o_refs, scale):
    y = _dot(x_ref[...], w_ref[...])
    gn = gn_ref[...]
    for hh in range(y.shape[1] // HEAD_DIM):
        sl = slice(hh * HEAD_DIM, (hh + 1) * HEAD_DIM)
        seg = y[:, sl]
        ms = jnp.mean(seg * seg, axis=-1, keepdims=True)
        z = seg * lax.rsqrt(ms + EPS) * gn
        if scale != 1.0:
            z = z * scale
        for o_ref in o_refs:
            o_ref[:, sl] = z.astype(o_ref.dtype)


def _matmul(body, x, w, extras, extra_specs, out_dtypes, *, tm, tn, name):
    M, K = x.shape
    N = w.shape[1]
    assert M % tm == 0 and N % tn == 0, (M, N, tm, tn)
    in_specs = [pl.BlockSpec((tm, K), lambda i, j: (i, 0)),
                pl.BlockSpec((K, tn), lambda i, j: (0, j))] + list(extra_specs)
    out_specs = [pl.BlockSpec((tm, tn), lambda i, j: (i, j)) for _ in out_dtypes]
    out_shape = [jax.ShapeDtypeStruct((M, N), dt) for dt in out_dtypes]
    block_bytes = (_nbytes((tm, K), x.dtype) + _nbytes((K, tn), w.dtype)
                   + sum(_nbytes((tm, tn), dt) for dt in out_dtypes)
                   + len(extras) * _nbytes((tm, tn), F32))
    outs = pl.pallas_call(
        body, grid=(M // tm, N // tn), in_specs=in_specs, out_specs=out_specs, out_shape=out_shape,
        compiler_params=_params(("parallel", "arbitrary"), block_bytes), name=name,
    )(x, w, *extras)
    return outs


MM_TM = 1024
MM_TN = 512


def _gate_spec(gate, tm, tn, rows_per_seq):
    if gate.ndim == 3:
        assert rows_per_seq % tm == 0
        per = rows_per_seq // tm
        return pl.BlockSpec((None, 1, tn), lambda i, j: (i // per, 0, j))
    return pl.BlockSpec((tm, tn), lambda i, j: (i, j))


def _mm_tiles(B, L, per_sequence_gate):
    rows = L if per_sequence_gate else B * L
    tm = MM_TM if rows % MM_TM == 0 else rows
    return tm, MM_TN


def _norm_mod(x_ref, g_ref, sc_ref, sh_ref):
    x = x_ref[...]
    ms = jnp.mean(x * x, axis=-1, keepdims=True)
    y = x * lax.rsqrt(ms + EPS) * g_ref[...]
    return y * (1.0 + sc_ref[...]) + sh_ref[...]


def _norm_kernel(x_ref, g_ref, sc_ref, sh_ref, h_ref):
    h_ref[...] = _norm_mod(x_ref, g_ref, sc_ref, sh_ref).astype(h_ref.dtype)


def _norm_router_kernel(x_ref, g_ref, sc_ref, sh_ref, wr_ref, br_ref, h_ref, gate_ref, idx_ref, *, n_experts):
    h = _norm_mod(x_ref, g_ref, sc_ref, sh_ref)
    h_ref[...] = h
    hi = h.astype(BF16)
    lo = (h - hi.astype(F32)).astype(BF16)
    w = wr_ref[...]
    whi = w.astype(BF16)
    wlo = (w - whi.astype(F32)).astype(BF16)
    logits = _dot(hi, whi) + _dot(hi, wlo) + _dot(lo, whi) + br_ref[...]
    lane = lax.broadcasted_iota(jnp.int32, logits.shape, 1)
    neg = jnp.float32(-jnp.inf)
    l = jnp.where(lane < n_experts, logits, neg)
    vals, idxs = [], []
    for _ in range(TOP_K):
        m = jnp.max(l, axis=-1, keepdims=True)
        idx = jnp.min(jnp.where(l == m, lane, V7X_LANES), axis=-1, keepdims=True)
        vals.append(m)
        idxs.append(idx)
        l = jnp.where(lane == idx, neg, l)
    exps = [jnp.exp(v - vals[0]) for v in vals]
    denom = exps[0]
    for e in exps[1:]:
        denom = denom + e
    gates = jnp.zeros(logits.shape, F32)
    sel = jnp.zeros(logits.shape, jnp.int32)
    for k in range(TOP_K):
        gates = jnp.where(lane == k, exps[k] / denom, gates)
        sel = jnp.where(lane == k, idxs[k], sel)
    gate_ref[...] = gates
    idx_ref[...] = sel


def _norm_rows(L):
    return 256 if L % 256 == 0 else L


def _norm_specs(B, L, D, tl):
    nl = L // tl
    x_spec = pl.BlockSpec((tl, D), lambda b, l: (b * nl + l, 0))
    g_spec = pl.BlockSpec((1, D), lambda b, l: (0, 0))
    m_spec = pl.BlockSpec((None, 1, D), lambda b, l: (b, 0, 0))
    return x_spec, g_spec, m_spec


def _norm(x, g, sc, sh, B, L, out_dtype):
    D = x.shape[1]
    tl = _norm_rows(L)
    x_spec, g_spec, m_spec = _norm_specs(B, L, D, tl)
    block_bytes = _nbytes((tl, D), F32) + _nbytes((tl, D), out_dtype) + 3 * _nbytes((1, D), F32)
    return pl.pallas_call(
        _norm_kernel, grid=(B, L // tl),
        in_specs=[x_spec, g_spec, m_spec, m_spec], out_specs=x_spec,
        out_shape=jax.ShapeDtypeStruct(x.shape, out_dtype),
        compiler_params=_params(("parallel", "parallel"), block_bytes), name="norm_mod",
    )(x, g.reshape(1, D), sc.reshape(B, 1, D), sh.reshape(B, 1, D))


def _norm_router(x, g, sc, sh, w_router, b_router, B, L):
    M, D = x.shape
    E = w_router.shape[1]
    tl = _norm_rows(L)
    x_spec, g_spec, m_spec = _norm_specs(B, L, D, tl)
    wr = jnp.pad(w_router, ((0, 0), (0, V7X_LANES - E)))
    br = jnp.pad(b_router, (0, V7X_LANES - E)).reshape(1, V7X_LANES)
    r_spec = pl.BlockSpec((tl, V7X_LANES), lambda b, l: (b * (L // tl) + l, 0))
    block_bytes = (2 * _nbytes((tl, D), F32) + 3 * _nbytes((1, D), F32) + _nbytes((D, V7X_LANES), F32)
                   + 2 * _nbytes((tl, V7X_LANES), F32))
    return pl.pallas_call(
        functools.partial(_norm_router_kernel, n_experts=E), grid=(B, L // tl),
        in_specs=[x_spec, g_spec, m_spec, m_spec,
                  pl.BlockSpec((D, V7X_LANES), lambda b, l: (0, 0)),
                  pl.BlockSpec((1, V7X_LANES), lambda b, l: (0, 0))],
        out_specs=[x_spec, r_spec, r_spec],
        out_shape=[jax.ShapeDtypeStruct((M, D), F32),
                   jax.ShapeDtypeStruct((M, V7X_LANES), F32),
                   jax.ShapeDtypeStruct((M, V7X_LANES), jnp.int32)],
        compiler_params=_params(("parallel", "parallel"), block_bytes), name="norm_router",
    )(x, g.reshape(1, D), sc.reshape(B, 1, D), sh.reshape(B, 1, D), wr, br)


def _hgrn_kernel(q_ref, f_ref, i_ref, og_ref, lb_ref, gn_ref, s0_ref, o_ref, sout_ref, s_scr, *, C, Hg):
    c = pl.program_id(2)
    nsub = C // SUB
    W = Hg * HEAD_DIM

    @pl.when(c == 0)
    def _():
        s_scr[...] = s0_ref[...]

    lb = lb_ref[...]
    f = lb + (1.0 - lb) * _sigmoid(f_ref[...])
    logf = jnp.log(f)
    kk = 1.0 - f
    row = lax.broadcasted_iota(jnp.int32, (C, C), 0)
    col = lax.broadcasted_iota(jnp.int32, (C, C), 1)
    tri = jnp.where(col <= row, 1.0, 0.0).astype(BF16)
    a1 = logf.astype(BF16)
    r1 = logf - a1.astype(F32)
    a2 = r1.astype(BF16)
    a3 = (r1 - a2.astype(F32)).astype(BF16)
    G = _dot(tri, a1) + _dot(tri, a2) + _dot(tri, a3)

    q = q_ref[...] * (HEAD_DIM ** -0.5)
    v = i_ref[...].astype(BF16)
    og = og_ref[...]
    gn = gn_ref[...]
    ends = [G[SUB * j + SUB - 1:SUB * j + SUB, :] for j in range(nsub)]
    bblk = jnp.concatenate([jnp.broadcast_to(e, (SUB, W)) for e in ends], axis=0)
    kt = kk * jnp.exp(bblk - G)
    k_end = (kt * jnp.exp(ends[-1] - bblk)).astype(BF16)
    kt = kt.astype(BF16)
    q_in = (q * jnp.exp(G)).astype(BF16)
    decay = jnp.exp(ends[-1])

    heads = [slice(h * HEAD_DIM, (h + 1) * HEAD_DIM) for h in range(Hg)]
    scores = []
    for sl in heads:
        row = []
        for j in range(nsub):
            r0 = SUB * j
            qj = (q[r0:, sl] * jnp.exp(G[r0:, sl] - ends[j][:, sl])).astype(BF16)
            row.append(_dot_nt(qj, kt[r0:r0 + SUB, sl]))
        scores.append(row)
    states = [s_scr[h] for h in range(Hg)]
    outs = [_dot(q_in[:, sl], states[h].astype(BF16)) for h, sl in enumerate(heads)]
    kvs = [_dot_tn(k_end[:, sl], v[:, sl]) for sl in heads]
    for h, sl in enumerate(heads):
        for j in range(nsub):
            r0 = SUB * j
            a = scores[h][j]
            rr = lax.broadcasted_iota(jnp.int32, a.shape, 0)
            ss = lax.broadcasted_iota(jnp.int32, a.shape, 1)
            a = jnp.where(rr >= ss, a, 0.0)
            part = _dot(a.astype(BF16), v[r0:r0 + SUB, sl])
            if r0:
                part = jnp.concatenate([jnp.zeros((r0, HEAD_DIM), F32), part], axis=0)
            outs[h] = outs[h] + part
    for h, sl in enumerate(heads):
        dcol = jnp.broadcast_to(decay[:, sl], (HEAD_DIM, HEAD_DIM)).T
        S_new = states[h] * dcol + kvs[h]
        s_scr[h] = S_new
        sout_ref[h] = S_new
        o = outs[h]
        ms = jnp.mean(o * o, axis=-1, keepdims=True)
        ogh = og[:, sl]
        y = o * lax.rsqrt(ms + EPS) * gn * (ogh * _sigmoid(ogh))
        o_ref[:, sl] = y.astype(o_ref.dtype)


def _hgrn(proj, lb, gnorm, s0, B, L):
    M, D4 = proj.shape
    D = D4 // 4
    H = D // HEAD_DIM
    C = CHUNK if L % CHUNK == 0 else L
    assert L % C == 0 and C % SUB == 0
    Hg = HGRN_HEADS
    W = Hg * HEAD_DIM
    nc, ng = L // C, H // Hg

    def sec_spec(sec):
        return pl.BlockSpec((C, W), lambda b, g, c: (b * nc + c, sec * ng + g))

    s_spec = pl.BlockSpec((None, Hg, HEAD_DIM, HEAD_DIM), lambda b, g, c: (b, g, 0, 0))
    block_bytes = 4 * _nbytes((C, W), F32) + _nbytes((C, W), BF16) + 2 * _nbytes((Hg, HEAD_DIM, HEAD_DIM), F32)
    o, s_out = pl.pallas_call(
        functools.partial(_hgrn_kernel, C=C, Hg=Hg), grid=(B, ng, nc),
        in_specs=[sec_spec(0), sec_spec(1), sec_spec(2), sec_spec(3),
                  pl.BlockSpec((1, W), lambda b, g, c: (0, g)),
                  pl.BlockSpec((1, HEAD_DIM), lambda b, g, c: (0, 0)),
                  s_spec],
        out_specs=[pl.BlockSpec((C, W), lambda b, g, c: (b * nc + c, g)), s_spec],
        out_shape=[jax.ShapeDtypeStruct((M, D), BF16),
                   jax.ShapeDtypeStruct((B, H, HEAD_DIM, HEAD_DIM), F32)],
        scratch_shapes=[pltpu.VMEM((Hg, HEAD_DIM, HEAD_DIM), F32)],
        compiler_params=_params(("parallel", "parallel", "arbitrary"), block_bytes,
                                _nbytes((Hg, HEAD_DIM, HEAD_DIM), F32)),
        name="hgrn2",
    )(proj, proj, proj, proj, lb.reshape(1, D), gnorm.reshape(1, HEAD_DIM), s0)
    return o, s_out


QT = 2 * CHUNK
BAND_T = BAND_ROWS + QT
ATTN_UNROLL = 4


def _attn_prompt_kernel(q_ref, k_ref, v_ref, bias_ref, o_ref, kpad, vpad):
    L = q_ref.shape[0]
    kpad[0:BAND_ROWS, :] = jnp.zeros((BAND_ROWS, HEAD_DIM), BF16)
    vpad[0:BAND_ROWS, :] = jnp.zeros((BAND_ROWS, HEAD_DIM), BF16)
    kpad[BAND_ROWS:, :] = k_ref[...]
    vpad[BAND_ROWS:, :] = v_ref[...]
    t = lax.broadcasted_iota(jnp.int32, (QT, BAND_T), 0)
    j = lax.broadcasted_iota(jnp.int32, (QT, BAND_T), 1)
    first = jnp.where(t >= CHUNK, CHUNK, 0)
    in_band = (j >= first) & (j < first + BAND_ROWS + CHUNK)
    bias = jnp.where(in_band, bias_ref[...], NEG_INF)

    def tile(i, carry):
        start = pl.multiple_of(i * QT, QT)
        q = q_ref[pl.ds(start, QT), :]
        kb = kpad[pl.ds(start, BAND_T), :]
        vb = vpad[pl.ds(start, BAND_T), :]
        s = _dot_nt(q, kb) + bias
        s = jnp.where(j >= BAND_ROWS - i * QT, s, NEG_INF)
        m = jnp.max(s, axis=-1, keepdims=True)
        p = jnp.exp(s - m)
        l = jnp.sum(p, axis=-1, keepdims=True)
        o = _dot(p.astype(BF16), vb) / l
        o_ref[pl.ds(start, QT), :] = o.astype(o_ref.dtype)
        return carry

    n_tiles = L // QT
    unroll = ATTN_UNROLL if n_tiles % ATTN_UNROLL == 0 else 1
    lax.fori_loop(0, n_tiles, tile, 0, unroll=unroll)


def _attn_prompt(q, k, v, bias, B, L):
    M, D = q.shape
    H = D // HEAD_DIM
    assert L % QT == 0
    spec = pl.BlockSpec((L, HEAD_DIM), lambda b, h: (b, h))
    block_bytes = 4 * _nbytes((L, HEAD_DIM), BF16) + _nbytes((QT, BAND_T), F32)
    scratch = 2 * _nbytes((L + BAND_ROWS, HEAD_DIM), BF16)
    return pl.pallas_call(
        _attn_prompt_kernel, grid=(B, H),
        in_specs=[spec, spec, spec, pl.BlockSpec((None, QT, BAND_T), lambda b, h: (h, 0, 0))],
        out_specs=spec, out_shape=jax.ShapeDtypeStruct((M, D), BF16),
        scratch_shapes=[pltpu.VMEM((L + BAND_ROWS, HEAD_DIM), BF16)] * 2,
        compiler_params=_params(("parallel", "parallel"), block_bytes, scratch), name="attn_prompt",
    )(q, k, v, bias)


def _attn_sample_kernel(q_ref, ck_ref, cv_ref, kn_ref, vn_ref, bias_ref, o_ref, *, Hg):
    T = q_ref.shape[0]
    J0 = ck_ref.shape[0]
    for h in range(Hg):
        sl = slice(h * HEAD_DIM, (h + 1) * HEAD_DIM)
        q = q_ref[:, sl]
        s1 = _dot_nt(q, ck_ref[:, h, :].astype(BF16)) + bias_ref[h, 0:T, 0:J0]
        s2 = _dot_nt(q, kn_ref[:, sl]) + bias_ref[h, 0:T, J0:J0 + T]
        m = jnp.maximum(jnp.max(s1, axis=-1, keepdims=True), jnp.max(s2, axis=-1, keepdims=True))
        p1 = jnp.exp(s1 - m)
        p2 = jnp.exp(s2 - m)
        l = jnp.sum(p1, axis=-1, keepdims=True) + jnp.sum(p2, axis=-1, keepdims=True)
        o = _dot(p1.astype(BF16), cv_ref[:, h, :].astype(BF16)) + _dot(p2.astype(BF16), vn_ref[:, sl])
        o_ref[:, sl] = (o / l).astype(o_ref.dtype)


def _attn_sample(q, cache_k, cache_v, k_new, v_new, bias, B, T):
    M, D = q.shape
    H = D // HEAD_DIM
    J0 = cache_k.shape[1]
    assert J0 == BAND_ROWS and T <= QT and J0 + T <= BAND_T
    Hg = 8
    W = Hg * HEAD_DIM
    n_spec = pl.BlockSpec((T, W), lambda b, g: (b, g))
    c_spec = pl.BlockSpec((None, J0, Hg, HEAD_DIM), lambda b, g: (b, 0, g, 0))
    block_bytes = (3 * _nbytes((T, W), BF16) + 2 * _nbytes((J0, W), F32) + _nbytes((Hg, QT, BAND_T), F32)
                   + _nbytes((T, W), BF16))
    return pl.pallas_call(
        functools.partial(_attn_sample_kernel, Hg=Hg), grid=(B, H // Hg),
        in_specs=[n_spec, c_spec, c_spec, n_spec, n_spec,
                  pl.BlockSpec((Hg, QT, BAND_T), lambda b, g: (g, 0, 0))],
        out_specs=n_spec, out_shape=jax.ShapeDtypeStruct((M, D), BF16),
        compiler_params=_params(("parallel", "parallel"), block_bytes), name="attn_sample",
    )(q, cache_k, cache_v, k_new, v_new, bias)


def _rel_bias_tile(table):
    strip_rel = QT - 1 + BAND_ROWS - jnp.arange(QT + BAND_T - 1)
    strip = table[:, jnp.clip(strip_rel, -(CHUNK - 1), REL_CLIP) + (CHUNK - 1)].astype(F32)
    rows = [lax.slice_in_dim(strip, QT - 1 - t, QT - 1 - t + BAND_T, axis=1) for t in range(QT)]
    return jnp.stack(rows, axis=1)


MOE_TM = 512
MOE_UP_TN = 512
MOE_DOWN_TN = 1024
ROUTE_CHUNK = 128
CAST_ROWS = 256
GATHER_ROWS = 256
COMBINE_ROWS = 128
ROW_DMA_UNROLL = 8


def _gather_kernel(idx_ref, nrows_ref, h_hbm, o_ref, buf, sem):
    R = o_ref.shape[0]
    i = pl.program_id(0)
    slot = lax.rem(i, 2)

    def issue(blk, slot_):
        def start(r, c):
            src = idx_ref[blk * R + r]
            pltpu.make_async_copy(h_hbm.at[pl.ds(src, 1)], buf.at[slot_, pl.ds(r, 1)], sem.at[slot_]).start()
            return c

        lax.fori_loop(0, R, start, 0, unroll=ROW_DMA_UNROLL)

    @pl.when(i == 0)
    def _():
        issue(0, 0)

    @pl.when((i + 1 < pl.num_programs(0)) & ((i + 1) * R < nrows_ref[0]))
    def _():
        issue(i + 1, 1 - slot)

    @pl.when(i * R < nrows_ref[0])
    def _():
        def wait(r, c):
            pltpu.make_async_copy(h_hbm.at[pl.ds(0, 1)], buf.at[slot, pl.ds(r, 1)], sem.at[slot]).wait()
            return c

        lax.fori_loop(0, R, wait, 0, unroll=ROW_DMA_UNROLL)
        o_ref[...] = buf[slot].astype(o_ref.dtype)

    @pl.when(i * R >= nrows_ref[0])
    def _():
        o_ref[...] = jnp.zeros(o_ref.shape, o_ref.dtype)


def _gather_rows(h, row_tok, n_used_rows):
    n_rows = row_tok.shape[0]
    D = h.shape[1]
    R = GATHER_ROWS
    assert n_rows % R == 0
    grid_spec = pltpu.PrefetchScalarGridSpec(
        num_scalar_prefetch=2, grid=(n_rows // R,),
        in_specs=[pl.BlockSpec(memory_space=pl.ANY)],
        out_specs=pl.BlockSpec((R, D), lambda i, idx, n: (i, 0)),
        scratch_shapes=[pltpu.VMEM((2, R, D), F32), pltpu.SemaphoreType.DMA((2,))])
    return pl.pallas_call(
        _gather_kernel, grid_spec=grid_spec, out_shape=jax.ShapeDtypeStruct((n_rows, D), BF16),
        compiler_params=_params(("arbitrary",), _nbytes((R, D), BF16), _nbytes((2, R, D), F32)),
        name="moe_gather",
    )(row_tok, n_used_rows, h)


def _refresh_weights(be_ref, nxt_ref, w_hbms, stage, wbf, sem, *, layer, tn):
    j = pl.program_id(0)
    i = pl.program_id(1)

    def tile_copy(k, e, jj):
        col = pl.multiple_of(jj * tn, tn)
        return pltpu.make_async_copy(w_hbms[k].at[layer, e, :, pl.ds(col, tn)], stage.at[k], sem.at[k])

    @pl.when((j == 0) & (i == 0))
    def _():
        for k in range(len(w_hbms)):
            tile_copy(k, be_ref[0], 0).start()

    nxt = nxt_ref[i]
    for k in range(len(w_hbms)):
        tile_copy(k, be_ref[i], j).wait()

        def cast_rows(c, carry):
            rows = pl.ds(pl.multiple_of(c * CAST_ROWS, CAST_ROWS), CAST_ROWS)
            wbf[k, rows, :] = stage[k, rows, :].astype(BF16)
            return carry

        lax.fori_loop(0, stage.shape[1] // CAST_ROWS, cast_rows, 0)

        @pl.when(nxt >= 0)
        def _():
            tile_copy(k, nxt, j).start()

        @pl.when((nxt < 0) & (j + 1 < pl.num_programs(0)))
        def _():
            tile_copy(k, be_ref[0], j + 1).start()


def _moe_up_kernel(be_ref, first_ref, nxt_ref, nblk_ref, x_ref, bg_ref, bu_ref, wg_hbm, wu_hbm, a_ref,
                   stage, wbf, sem, *, layer, tn):
    i = pl.program_id(1)

    @pl.when(i < nblk_ref[0])
    def _():
        @pl.when(first_ref[i] == 1)
        def _():
            _refresh_weights(be_ref, nxt_ref, (wg_hbm, wu_hbm), stage, wbf, sem, layer=layer, tn=tn)

        x = x_ref[...]
        g = jnp.minimum(_dot(x, wbf[0]) + bg_ref[...], SWIGLU_LIMIT)
        u = jnp.clip(_dot(x, wbf[1]) + bu_ref[...], -SWIGLU_LIMIT, SWIGLU_LIMIT)
        a_ref[...] = (g * _sigmoid(SWIGLU_ALPHA * g) * (u + 1.0)).astype(a_ref.dtype)

    @pl.when(i >= nblk_ref[0])
    def _():
        a_ref[...] = jnp.zeros(a_ref.shape, a_ref.dtype)


def _moe_down_kernel(be_ref, first_ref, nxt_ref, nblk_ref, a_ref, bd_ref, wd_hbm, y_ref, stage, wbf, sem,
                     *, layer, tn):
    i = pl.program_id(1)

    @pl.when(i < nblk_ref[0])
    def _():
        @pl.when(first_ref[i] == 1)
        def _():
            _refresh_weights(be_ref, nxt_ref, (wd_hbm,), stage, wbf, sem, layer=layer, tn=tn)

        y_ref[...] = _dot(a_ref[...], wbf[0]) + bd_ref[...]

    @pl.when(i >= nblk_ref[0])
    def _():
        y_ref[...] = jnp.zeros(y_ref.shape, y_ref.dtype)


def _moe_grouped(body, x, weights, biases, layer, blk_e, blk_first, blk_next, n_blk, out_dtype, tn, name):
    n_rows, K = x.shape
    N = weights[0].shape[-1]
    tm = MOE_TM
    n_blocks = n_rows // tm
    nw = len(weights)

    def x_map(j, i, be, fi, nx, nb):
        return (jnp.minimum(i, nb[0] - 1), 0)

    def b_map(j, i, be, fi, nx, nb):
        return (layer, be[i], 0, j)

    in_specs = [pl.BlockSpec((tm, K), x_map)]
    in_specs += [pl.BlockSpec((None, None, 1, tn), b_map) for _ in biases]
    in_specs += [pl.BlockSpec(memory_space=pl.ANY) for _ in weights]
    grid_spec = pltpu.PrefetchScalarGridSpec(
        num_scalar_prefetch=4, grid=(N // tn, n_blocks), in_specs=in_specs,
        out_specs=pl.BlockSpec((tm, tn), lambda j, i, be, fi, nx, nb: (i, j)),
        scratch_shapes=[pltpu.VMEM((nw, K, tn), F32), pltpu.VMEM((nw, K, tn), BF16),
                        pltpu.SemaphoreType.DMA((nw,))])
    block_bytes = _nbytes((tm, K), x.dtype) + _nbytes((tm, tn), out_dtype) + len(biases) * _nbytes((8, tn), F32)
    scratch = nw * (_nbytes((K, tn), F32) + _nbytes((K, tn), BF16)) + (nw + 2) * _nbytes((tm, tn), F32)
    b4 = [b.reshape(b.shape[0], b.shape[1], 1, b.shape[2]) for b in biases]
    return pl.pallas_call(
        functools.partial(body, layer=layer, tn=tn), grid_spec=grid_spec,
        out_shape=jax.ShapeDtypeStruct((n_rows, N), out_dtype),
        compiler_params=_params(("arbitrary", "arbitrary"), block_bytes, scratch), name=name,
    )(blk_e, blk_first, blk_next, n_blk, x, *b4, *weights)


def _combine_kernel(pos_ref, y_hbm, gates_ref, x_ref, gt_ref, o_ref, buf, sem):
    R = x_ref.shape[0]
    i = pl.program_id(0)
    slot = lax.rem(i, 2)

    def row_copy(slot_, k, r, src_row):
        return pltpu.make_async_copy(y_hbm.at[pl.ds(src_row, 1)], buf.at[slot_, k, pl.ds(r, 1)], sem.at[slot_])

    def issue(blk, slot_):
        def start(r, c):
            for k in range(TOP_K):
                row_copy(slot_, k, r, pos_ref[(blk * R + r) * TOP_K + k]).start()
            return c

        lax.fori_loop(0, R, start, 0, unroll=ROW_DMA_UNROLL // TOP_K)

    @pl.when(i == 0)
    def _():
        issue(0, 0)

    @pl.when(i + 1 < pl.num_programs(0))
    def _():
        issue(i + 1, 1 - slot)

    def wait(r, c):
        for k in range(TOP_K):
            row_copy(slot, k, r, 0).wait()
        return c

    lax.fori_loop(0, R, wait, 0, unroll=ROW_DMA_UNROLL // TOP_K)
    gates = gates_ref[...]
    moe = gates[:, 0:1] * buf[slot, 0]
    for k in range(1, TOP_K):
        moe = moe + gates[:, k:k + 1] * buf[slot, k]
    o_ref[...] = x_ref[...] + gt_ref[...] * moe


def _combine(y_sorted, pos, gates, x, gt, rows_per_seq):
    T, D = x.shape
    R = COMBINE_ROWS if T % COMBINE_ROWS == 0 else T
    if gt.ndim == 3:
        assert rows_per_seq % R == 0
        per = rows_per_seq // R
        gt_spec = pl.BlockSpec((None, 1, D), lambda i, p: (i // per, 0, 0))
    else:
        gt_spec = pl.BlockSpec((R, D), lambda i, p: (i, 0))
    row_spec = pl.BlockSpec((R, D), lambda i, p: (i, 0))
    grid_spec = pltpu.PrefetchScalarGridSpec(
        num_scalar_prefetch=1, grid=(T // R,),
        in_specs=[pl.BlockSpec(memory_space=pl.ANY),
                  pl.BlockSpec((R, V7X_LANES), lambda i, p: (i, 0)), row_spec, gt_spec],
        out_specs=row_spec,
        scratch_shapes=[pltpu.VMEM((2, TOP_K, R, D), F32), pltpu.SemaphoreType.DMA((2,))])
    block_bytes = 3 * _nbytes((R, D), F32) + _nbytes((R, V7X_LANES), F32)
    return pl.pallas_call(
        _combine_kernel, grid_spec=grid_spec, out_shape=jax.ShapeDtypeStruct((T, D), F32),
        compiler_params=_params(("arbitrary",), block_bytes, _nbytes((2, TOP_K, R, D), F32)),
        name="moe_combine",
    )(pos, y_sorted, gates, x, gt)


def _route(top_e, n_experts):
    A = top_e.shape[0] * TOP_K
    tm = MOE_TM
    e_flat = top_e.reshape(A)
    assert A % ROUTE_CHUNK == 0 and A < 2 ** 24
    onehot = (e_flat[:, None] == jnp.arange(n_experts, dtype=jnp.int32)[None, :]).astype(F32)
    oh = onehot.reshape(A // ROUTE_CHUNK, ROUTE_CHUNK, n_experts)
    tri = jnp.tril(jnp.ones((ROUTE_CHUNK, ROUTE_CHUNK), F32))
    local = jnp.einsum('ts,csk->ctk', tri, oh, preferred_element_type=F32)
    tot = local[:, -1, :]
    offs = jnp.cumsum(tot, axis=0) - tot
    counts = (offs[-1] + tot[-1]).astype(jnp.int32)
    rank = (jnp.sum(oh * (local + offs[:, None, :]), axis=-1).reshape(A) - 1.0).astype(jnp.int32)
    padded = (counts + tm - 1) // tm * tm
    pend = jnp.cumsum(padded)
    pstart = pend - padded
    pos = jnp.sum(onehot.astype(jnp.int32) * pstart[None, :], axis=-1) + rank
    n_blocks = -(-A // tm) + n_experts
    n_blocks = -(-n_blocks * tm // GATHER_ROWS) * GATHER_ROWS // tm
    n_rows = n_blocks * tm
    row_tok = jnp.zeros((n_rows,), jnp.int32).at[pos].set(jnp.arange(A, dtype=jnp.int32) // TOP_K)
    blk_start = jnp.arange(n_blocks, dtype=jnp.int32) * tm
    blk_e = jnp.minimum(jnp.sum((blk_start[:, None] >= pend[None, :]).astype(jnp.int32), axis=1),
                        n_experts - 1).astype(jnp.int32)
    blk_first = jnp.concatenate([jnp.ones((1,), jnp.int32), (blk_e[1:] != blk_e[:-1]).astype(jnp.int32)])
    n_used = pend[-1:].astype(jnp.int32)
    n_blk = n_used // tm
    blk_id = jnp.arange(n_blocks, dtype=jnp.int32)
    run_start = jnp.where((blk_first == 1) & (blk_id < n_blk[0]), blk_id, n_blocks)
    later = lax.cummin(run_start, axis=0, reverse=True)
    nxt_id = jnp.concatenate([later[1:], jnp.full((1,), n_blocks, jnp.int32)])
    blk_next = jnp.where(nxt_id < n_blocks, blk_e[jnp.minimum(nxt_id, n_blocks - 1)], -1).astype(jnp.int32)
    return pos, row_tok, blk_e, blk_first, blk_next, n_blk, n_used


def _moe(h, top_e, layer, w_gate, b_gate, w_up, b_up, w_down, b_down):
    E = w_gate.shape[1]
    pos, row_tok, blk_e, blk_first, blk_next, n_blk, n_used = _route(top_e, E)
    xs = _gather_rows(h, row_tok, n_used)
    a = _moe_grouped(_moe_up_kernel, xs, [w_gate, w_up], [b_gate, b_up], layer, blk_e, blk_first, blk_next,
                     n_blk, BF16, MOE_UP_TN, "moe_up")
    y = _moe_grouped(_moe_down_kernel, a, [w_down], [b_down], layer, blk_e, blk_first, blk_next,
                     n_blk, F32, MOE_DOWN_TN, "moe_down")
    return y, pos


def kernel(x_prompt, x_sample, c_prompt, c_sample, state_hgrn, cache_k, cache_v, norm_mix, norm_ffn, w_ada, b_ada, w_in_a, lb_param, gnorm_a, w_out_a, kv_norm, w_kv, k_norm, w_q_b, q_norm, rel_bias, w_out_b, w_router, b_router, w_gate, b_gate, w_up, b_up, w_down, b_down):
    Bp, Lp, D = x_prompt.shape
    Bs, Ls, _ = x_sample.shape
    depth = norm_mix.shape[0]
    n_a = w_in_a.shape[0]
    H = D // HEAD_DIM
    Mp, Ms = Bp * Lp, Bs * Ls
    streams = ((Bp, Lp), (Bs, Ls))

    cond = jnp.concatenate([c_prompt, c_sample], axis=0)
    cond = (cond * jax.nn.sigmoid(cond)).astype(BF16)
    n_cond, n_mod = cond.shape[0], w_ada.shape[2]
    mod = pl.pallas_call(
        _mm_bias_kernel, grid=(depth, n_mod // MM_TN),
        in_specs=[pl.BlockSpec((n_cond, D), lambda l, j: (0, 0)),
                  pl.BlockSpec((None, D, MM_TN), lambda l, j: (l, 0, j)),
                  pl.BlockSpec((None, 1, MM_TN), lambda l, j: (l, 0, j))],
        out_specs=pl.BlockSpec((None, n_cond, MM_TN), lambda l, j: (l, 0, j)),
        out_shape=jax.ShapeDtypeStruct((depth, n_cond, n_mod), F32),
        compiler_params=_params(("parallel", "arbitrary"),
                                _nbytes((D, MM_TN), F32) + _nbytes((n_cond, D), BF16) + _nbytes((n_cond, MM_TN), F32)),
        name="ada",
    )(cond, w_ada, b_ada.reshape(depth, 1, n_mod))
    mods = [jnp.split(mod[l], 6, axis=-1) for l in range(depth)]

    def per_stream(arr, s):
        return arr[:Bp] if s == 0 else arr[Bp:]

    def gate_for(arr, s):
        g = per_stream(arr, s)
        if s == 0:
            return g.reshape(Bp, 1, D)
        return jnp.repeat(g, Ls, axis=0)

    lb_all = jnp.cumsum(jax.nn.softmax(lb_param.astype(F32), axis=0), axis=0)
    xs = [x_prompt.reshape(Mp, D), x_sample.reshape(Ms, D)]
    s0s = [jnp.zeros((n_a, Bp, H, HEAD_DIM, HEAD_DIM), state_hgrn.dtype), state_hgrn]
    states = [[], []]
    k32 = [None, None]
    v32 = [None, None]
    k16 = [None, None]
    v16 = [None, None]
    zeros_mod = [jnp.zeros((Bp, D), F32), jnp.zeros((Bs, D), F32)]

    for l in range(depth):
        sh1, sc1, gt1, sh2, sc2, gt2 = mods[l]
        for s, (B, L) in enumerate(streams):
            tm, tn = _mm_tiles(B, L, s == 0)
            h = _norm(xs[s], norm_mix[l], per_stream(sc1, s), per_stream(sh1, s), B, L, BF16)
            if l < n_a:
                (proj,) = _matmul(_mm_plain_kernel, h, w_in_a[l].astype(BF16), [], [], [F32],
                                  tm=tm, tn=tn, name="hgrn_in")
                o, S = _hgrn(proj, lb_all[l], gnorm_a[l], s0s[s][l].astype(F32), B, L)
                states[s].append(S.astype(state_hgrn.dtype))
                w_o = w_out_a[l]
            else:
                jb = l - n_a
                (q,) = _matmul(functools.partial(_mm_headnorm_kernel, scale=HEAD_DIM ** -0.5), h,
                               w_q_b[jb].astype(BF16), [q_norm[jb].reshape(1, HEAD_DIM)],
                               [pl.BlockSpec((1, HEAD_DIM), lambda i, j: (0, 0))], [BF16],
                               tm=tm, tn=tn, name="q_proj")
                bias_tile = _rel_bias_tile(rel_bias[jb])
                if s == 0:
                    o = _attn_prompt(q, k16[s], v16[s], bias_tile, B, L)
                else:
                    o = _attn_sample(q, cache_k, cache_v, k16[s], v16[s], bias_tile, B, L)
                w_o = w_out_b[jb]
            g1 = gate_for(gt1, s)
            (xs[s],) = _matmul(_mm_resgate_kernel, o, w_o.astype(BF16), [xs[s], g1],
                               [pl.BlockSpec((tm, tn), lambda i, j: (i, j)), _gate_spec(g1, tm, tn, L)],
                               [F32], tm=tm, tn=tn, name="out_proj")
        hs, gs, es = [], [], []
        for s, (B, L) in enumerate(streams):
            h, gates, top_e = _norm_router(xs[s], norm_ffn[l], per_stream(sc2, s), per_stream(sh2, s),
                                           w_router[l], b_router[l], B, L)
            hs.append(h)
            gs.append(gates)
            es.append(top_e[:, :TOP_K])
        y_sorted, pos = _moe(jnp.concatenate(hs, axis=0), jnp.concatenate(es, axis=0), l,
                             w_gate, b_gate, w_up, b_up, w_down, b_down)
        pos_s = [pos[:Mp * TOP_K], pos[Mp * TOP_K:]]
        for s, (B, L) in enumerate(streams):
            xs[s] = _combine(y_sorted, pos_s[s], gs[s], xs[s], gate_for(gt2, s), L)
        if l == n_a - 1:
            w_k = w_kv[:, :D].astype(BF16)
            w_v = w_kv[:, D:].astype(BF16)
            for s, (B, L) in enumerate(streams):
                tm, tn = _mm_tiles(B, L, s == 0)
                hn = _norm(xs[s], kv_norm, zeros_mod[s], zeros_mod[s], B, L, BF16)
                k32[s], k16[s] = _matmul(functools.partial(_mm_headnorm_kernel, scale=1.0), hn, w_k,
                                         [k_norm.reshape(1, HEAD_DIM)],
                                         [pl.BlockSpec((1, HEAD_DIM), lambda i, j: (0, 0))], [F32, BF16],
                                         tm=tm, tn=tn, name="k_proj")
                v32[s], v16[s] = _matmul(_mm_plain_kernel, hn, w_v, [], [], [F32, BF16],
                                         tm=tm, tn=tn, name="v_proj")

    rows = min(BAND_ROWS, Lp)
    kp = k32[0].reshape(Bp, Lp, D)[:, Lp - rows:].reshape(Bp, rows, H, HEAD_DIM)
    vp = v32[0].reshape(Bp, Lp, D)[:, Lp - rows:].reshape(Bp, rows, H, HEAD_DIM)
    return (xs[0].reshape(Bp, Lp, D), xs[1].reshape(Bs, Ls, D),
            jnp.stack(states[0]), jnp.stack(states[1]),
            kp, vp,
            k32[1].reshape(Bs, Ls, H, HEAD_DIM), v32[1].reshape(Bs, Ls, H, HEAD_DIM))
```
